```python
import math
import jax, jax.numpy as jnp
from jax import lax
import numpy as np

D_MODEL = 1024
BATCH = 4
SEQ = 4096
DEPTH = 2

HEAD_DIM = 64
N_HEADS_A = D_MODEL // (2 * HEAD_DIM)
N_HEADS_B = D_MODEL // (2 * HEAD_DIM)
N_KV_B = max(1, N_HEADS_B // 4)
GROUP_B = N_HEADS_B // N_KV_B
MIX_WIDTH = (N_HEADS_A + N_HEADS_B) * HEAD_DIM
DILATED_CONFIGS = ((128, 1), (512, 4), (2048, 16))
SWA_RADIUS = 128
N_BUCKETS = 32
MAX_DISTANCE = 1024
D_FF = ((8 * D_MODEL // 3 + 127) // 128) * 128
PLE_DIM = 256
EPS = 1e-6
QBLOCK = 128
NEG = -1e30

QKV_WIDTHS = (N_HEADS_A * HEAD_DIM, N_HEADS_A * HEAD_DIM, N_HEADS_A * HEAD_DIM,
              N_HEADS_B * HEAD_DIM, N_KV_B * HEAD_DIM, N_KV_B * HEAD_DIM)
QKV_WIDTH = sum(QKV_WIDTHS)

kernel_name = "hybrid_dilated_swa_macaron_encoder"


def rms_norm(x, g):
    xf = x.astype(jnp.float32)
    y = xf * lax.rsqrt(jnp.mean(xf * xf, axis=-1, keepdims=True) + EPS) * g.astype(jnp.float32)
    return y.astype(x.dtype)


def swiglu(h, w_in, w_out):
    gate, up = jnp.split(h @ w_in, 2, axis=-1)
    return (jax.nn.silu(gate) * up) @ w_out


def t5_bucket(rel):
    half = N_BUCKETS // 2
    max_exact = half // 2
    ret = jnp.where(rel > 0, half, 0)
    n = jnp.abs(rel)
    nf = jnp.maximum(n, 1).astype(jnp.float32)
    large = max_exact + (jnp.log(nf / max_exact) / math.log(MAX_DISTANCE / max_exact)
                         * (half - max_exact)).astype(jnp.int32)
    large = jnp.minimum(large, half - 1)
    return ret + jnp.where(n < max_exact, n, large)


def banded_attention(q, k, v, radius, dilation, bias_heads, sink=None):
    N, Hk, G, L, E = q.shape
    bq = math.gcd(L, QBLOCK)
    nb = L // bq
    W = bq + 2 * radius
    pad = ((0, 0), (0, 0), (radius, radius), (0, 0))
    idx = (jnp.arange(nb) * bq)[:, None] + jnp.arange(W)[None, :]
    kb = jnp.pad(k, pad)[:, :, idx]
    vb = jnp.pad(v, pad)[:, :, idx]
    qb = q.reshape(N, Hk, G, nb, bq, E)
    logits = jnp.einsum('nhgbqe,nhbke->nhgbqk', qb, kb,
                        preferred_element_type=jnp.float32) * (E ** -0.5)
    rel = jnp.arange(W)[None, :] - radius - jnp.arange(bq)[:, None]
    bias = bias_heads[t5_bucket(rel * dilation)].astype(jnp.float32)
    bias = jnp.transpose(bias.reshape(bq, W, Hk, G), (2, 3, 0, 1))[:, :, None]
    in_band = jnp.abs(rel) <= radius
    in_seq = (idx >= radius) & (idx < radius + L)
    valid = in_band[None] & in_seq[:, None, :]
    logits = jnp.where(valid, logits + bias, NEG)
    m = jnp.max(logits, axis=-1)
    if sink is not None:
        sink_b = sink.astype(jnp.float32)[None, :, :, None, None]
        m = jnp.maximum(m, sink_b)
    pr = jnp.exp(logits - m[..., None])
    denom = jnp.sum(pr, axis=-1)
    if sink is not None:
        denom = denom + jnp.exp(sink_b - m)
    out = jnp.einsum('nhgbqk,nhbke->nhgbqe', pr.astype(v.dtype), vb,
                     preferred_element_type=jnp.float32) / denom[..., None]
    lse = m + jnp.log(denom)
    return out.reshape(N, Hk, G, L, E).astype(q.dtype), lse.reshape(N, Hk, G, L)


def dilated_attention(q, k, v, bias_heads):
    B, H, S, E = q.shape
    outs, lses = [], []
    for window, d in DILATED_CONFIGS:
        L = S // d
        def split(t):
            return t.reshape(B, H, L, d, E).transpose(0, 3, 1, 2, 4).reshape(B * d, H, L, E)
        o, lse = banded_attention(split(q)[:, :, None], split(k), split(v),
                                  window // (2 * d), d, bias_heads)
        outs.append(o[:, :, 0].reshape(B, d, H, L, E).transpose(0, 2, 3, 1, 4).reshape(B, H, S, E))
        lses.append(lse[:, :, 0].reshape(B, d, H, L).transpose(0, 2, 3, 1).reshape(B, H, S))
    wts = jax.nn.softmax(jnp.stack(lses), axis=0)
    return jnp.einsum('cbhs,cbhse->bhse', wts, jnp.stack(outs).astype(jnp.float32)).astype(q.dtype)


def setup_inputs(seed: int = 0) -> dict:
    key = jax.random.key(seed)
    ks = jax.random.split(key, 20)
    f32 = jnp.float32

    def nrm(k, shape):
        return jax.random.normal(k, shape, f32)

    def w(k, shape, fan_in):
        return nrm(k, shape) * fan_in ** -0.5

    def gain(k, shape):
        return 1.0 + 0.05 * nrm(k, shape)

    return {
        "x": nrm(ks[0], (BATCH, SEQ, D_MODEL)),
        "p": nrm(ks[1], (DEPTH, BATCH, SEQ, PLE_DIM)),
        "rel_bias": 0.5 * nrm(ks[2], (N_BUCKETS, N_HEADS_A + N_HEADS_B)),
        "norm_ffn1": gain(ks[3], (DEPTH, D_MODEL)),
        "ffn1_w_in": w(ks[4], (DEPTH, D_MODEL, 2 * D_FF), D_MODEL),
        "ffn1_w_out": w(ks[5], (DEPTH, D_FF, D_MODEL), D_FF),
        "norm_mix": gain(ks[6], (DEPTH, D_MODEL)),
        "w_qkv": w(ks[7], (DEPTH, D_MODEL, QKV_WIDTH), D_MODEL),
        "q_norm_a": gain(ks[8], (DEPTH, HEAD_DIM)),
        "k_norm_a": gain(ks[9], (DEPTH, HEAD_DIM)),
        "q_norm_b": gain(ks[10], (DEPTH, HEAD_DIM)),
        "k_norm_b": gain(ks[11], (DEPTH, HEAD_DIM)),
        "sink_b": 0.5 * nrm(ks[12], (DEPTH, N_HEADS_B)),
        "w_o": w(ks[13], (DEPTH, MIX_WIDTH, D_MODEL), MIX_WIDTH),
        "norm_ffn2": gain(ks[14], (DEPTH, D_MODEL)),
        "ffn2_w_in": w(ks[15], (DEPTH, D_MODEL, 2 * D_FF), D_MODEL),
        "ffn2_w_out": w(ks[16], (DEPTH, D_FF, D_MODEL), D_FF),
        "norm_ple": gain(ks[17], (DEPTH, D_MODEL)),
        "w_ple_gate": w(ks[18], (DEPTH, D_MODEL, D_MODEL), D_MODEL),
        "w_ple_proj": w(ks[19], (DEPTH, PLE_DIM, D_MODEL), PLE_DIM),
    }


def reference(x, p, rel_bias, norm_ffn1, ffn1_w_in, ffn1_w_out, norm_mix, w_qkv,
              q_norm_a, k_norm_a, q_norm_b, k_norm_b, sink_b, w_o, norm_ffn2,
              ffn2_w_in, ffn2_w_out, norm_ple, w_ple_gate, w_ple_proj):
    B, S, _ = x.shape
    split_at = [int(c) for c in np.cumsum(QKV_WIDTHS)[:-1]]
    bias_a = rel_bias[:, :N_HEADS_A]
    bias_b = rel_bias[:, N_HEADS_A:]

    def heads(t, n):
        return t.reshape(B, S, n, HEAD_DIM).transpose(0, 2, 1, 3)

    for i in range(DEPTH):
        x = x + 0.5 * swiglu(rms_norm(x, norm_ffn1[i]), ffn1_w_in[i], ffn1_w_out[i])

        h = rms_norm(x, norm_mix[i])
        qa, ka, va, qb, kb, vb = jnp.split(h @ w_qkv[i], split_at, axis=-1)

        qa = rms_norm(heads(qa, N_HEADS_A), q_norm_a[i])
        ka = rms_norm(heads(ka, N_HEADS_A), k_norm_a[i])
        oa = dilated_attention(qa, ka, heads(va, N_HEADS_A), bias_a)

        qb = rms_norm(heads(qb, N_HEADS_B), q_norm_b[i]).reshape(B, N_KV_B, GROUP_B, S, HEAD_DIM)
        kb = rms_norm(heads(kb, N_KV_B), k_norm_b[i])
        ob, _ = banded_attention(qb, kb, heads(vb, N_KV_B), SWA_RADIUS, 1, bias_b,
                                 sink_b[i].reshape(N_KV_B, GROUP_B))
        ob = ob.reshape(B, N_HEADS_B, S, HEAD_DIM)

        o = jnp.concatenate([oa, ob], axis=1).transpose(0, 2, 1, 3).reshape(B, S, MIX_WIDTH)
        x = x + o @ w_o[i]

        x = x + 0.5 * swiglu(rms_norm(x, norm_ffn2[i]), ffn2_w_in[i], ffn2_w_out[i])

        gate = jax.nn.sigmoid(rms_norm(x, norm_ple[i]) @ w_ple_gate[i])
        x = x + gate * (p[i] @ w_ple_proj[i])
    return x
```

```python
import functools
import math

import jax
import jax.numpy as jnp
import numpy as np
from jax import lax
from jax.experimental import pallas as pl
from jax.experimental.pallas import tpu as pltpu

HEAD_DIM = 64
N_HEADS = 8
N_KV_B = 2
GROUP_B = N_HEADS // N_KV_B
MIX_A = N_HEADS * HEAD_DIM
KV_B = N_KV_B * HEAD_DIM
DILATED_CONFIGS = ((128, 1), (512, 4), (2048, 16))
SWA_RADIUS = 128
N_BUCKETS = 32
MAX_DISTANCE = 1024
EPS = 1e-6
NEG = -1e30
QBLOCK = 128
LSE_LANES = 16
LANE = 128

VMEM_LIMIT = 48 * 1024 * 1024

BF16 = jnp.bfloat16
F32 = jnp.float32


def _cparams(sem):
    return pltpu.CompilerParams(dimension_semantics=sem, vmem_limit_bytes=VMEM_LIMIT)


def _rms_scale(x):
    return lax.rsqrt(jnp.mean(x * x, axis=-1, keepdims=True) + EPS)


def _ffn_kernel(x_ref, g_ref, wg_ref, wu_ref, wo_ref, o_ref, h_ref, acc_ref, *, nf):
    j = pl.program_id(1)

    @pl.when(j == 0)
    def _():
        x = x_ref[...]
        h_ref[...] = (x * _rms_scale(x) * g_ref[...]).astype(BF16)
        acc_ref[...] = jnp.zeros_like(acc_ref)

    h = h_ref[...]
    gate = jnp.dot(h, wg_ref[...], preferred_element_type=F32)
    up = jnp.dot(h, wu_ref[...], preferred_element_type=F32)
    act = (gate * jax.nn.sigmoid(gate) * up).astype(BF16)
    acc_ref[...] += jnp.dot(act, wo_ref[...], preferred_element_type=F32)

    @pl.when(j == nf - 1)
    def _():
        o_ref[...] = x_ref[...] + 0.5 * acc_ref[...]


def _ffn(x, g, w_in, w_out, *, tm=1024, tf=256):
    T, D = x.shape
    d_ff = w_out.shape[0]
    nf = d_ff // tf
    return pl.pallas_call(
        functools.partial(_ffn_kernel, nf=nf),
        grid=(T // tm, nf),
        in_specs=[
            pl.BlockSpec((tm, D), lambda i, j: (i, 0)),
            pl.BlockSpec((1, D), lambda i, j: (0, 0)),
            pl.BlockSpec((D, tf), lambda i, j: (0, j)),
            pl.BlockSpec((D, tf), lambda i, j: (0, j + nf)),
            pl.BlockSpec((tf, D), lambda i, j: (j, 0)),
        ],
        out_specs=pl.BlockSpec((tm, D), lambda i, j: (i, 0)),
        out_shape=jax.ShapeDtypeStruct((T, D), F32),
        scratch_shapes=[pltpu.VMEM((tm, D), BF16), pltpu.VMEM((tm, D), F32)],
        compiler_params=_cparams(("parallel", "arbitrary")),
    )(x, g, w_in, w_in, w_out)


def _head_mean_sq(y):
    sq = y * y
    hi = sq.astype(BF16)
    lo = (sq - hi.astype(F32)).astype(BF16)
    r = lax.broadcasted_iota(jnp.int32, (LANE, LANE), 0) // HEAD_DIM
    c = lax.broadcasted_iota(jnp.int32, (LANE, LANE), 1) // HEAD_DIM
    seg = jnp.where(r == c, 1.0 / HEAD_DIM, 0.0).astype(BF16)
    return (jnp.dot(hi, seg, preferred_element_type=F32)
            + jnp.dot(lo, seg, preferred_element_type=F32))


def _qkv_kernel(x_ref, g_ref, w_ref, gq_ref, qa_ref, ka_ref, va_ref, qb_ref, kb_ref, vb_ref):
    x = x_ref[...]
    h = (x * _rms_scale(x) * g_ref[...]).astype(BF16)
    col = 0
    for ref, normed in ((qa_ref, True), (ka_ref, True), (va_ref, False),
                        (qb_ref, True), (kb_ref, True), (vb_ref, False)):
        width = ref.shape[-1]
        y = jnp.dot(h, w_ref[:, col:col + width], preferred_element_type=F32)
        if normed:
            for c in range(0, width, LANE):
                yc = y[:, c:c + LANE]
                gain = gq_ref[:, col + c:col + c + LANE]
                ref[:, c:c + LANE] = (yc * lax.rsqrt(_head_mean_sq(yc) + EPS) * gain).astype(BF16)
        else:
            ref[...] = y.astype(BF16)
        col += width


def _qkv(x, g, w, gains, *, tm=512):
    T, D = x.shape
    widths = (MIX_A, MIX_A, MIX_A, MIX_A, KV_B, KV_B)
    return pl.pallas_call(
        _qkv_kernel,
        grid=(T // tm,),
        in_specs=[
            pl.BlockSpec((tm, D), lambda i: (i, 0)),
            pl.BlockSpec((1, D), lambda i: (0, 0)),
            pl.BlockSpec(w.shape, lambda i: (0, 0)),
            pl.BlockSpec(gains.shape, lambda i: (0, 0)),
        ],
        out_specs=[pl.BlockSpec((tm, wd), lambda i: (i, 0)) for wd in widths],
        out_shape=[jax.ShapeDtypeStruct((T, wd), BF16) for wd in widths],
        compiler_params=_cparams(("parallel",)),
    )(x, g, w, gains)


def _t5_bucket_np(rel):
    half = N_BUCKETS // 2
    max_exact = half // 2
    ret = np.where(rel > 0, half, 0)
    n = np.abs(rel)
    nf = np.maximum(n, 1).astype(np.float64)
    large = max_exact + (np.log(nf / max_exact) / math.log(MAX_DISTANCE / max_exact)
                         * (half - max_exact)).astype(np.int64)
    large = np.minimum(large, half - 1)
    return ret + np.where(n < max_exact, n, large)


def _bucket_tiles(radius, dilation):
    w = QBLOCK + 2 * radius
    q = np.arange(QBLOCK)[:, None]
    j = np.arange(w)[None, :]
    tiles = []
    for off in (0, -radius, -2 * radius):
        rel = off + j - q
        tiles.append(np.where(np.abs(rel) <= radius, _t5_bucket_np(rel * dilation), -1))
    return np.stack(tiles).astype(np.int32)


def _bias_kernel(rb_ref, idx_ref, o_ref, *, head0):
    h = pl.program_id(1) + head0
    idx = idx_ref[...]
    tile = jnp.full(idx.shape, NEG, F32)
    for b in range(N_BUCKETS):
        tile = jnp.where(idx == b, rb_ref[b, h], tile)
    o_ref[...] = tile


def _bias_tiles(rel_bias, radius, dilation, head0):
    idx = jnp.asarray(_bucket_tiles(radius, dilation))
    nv, bq, w = idx.shape
    return pl.pallas_call(
        functools.partial(_bias_kernel, head0=head0),
        grid=(nv, N_HEADS),
        in_specs=[
            pl.BlockSpec(memory_space=pltpu.SMEM),
            pl.BlockSpec((None, bq, w), lambda v, h: (v, 0, 0)),
        ],
        out_specs=pl.BlockSpec((None, None, bq, w), lambda v, h: (v, h, 0, 0)),
        out_shape=jax.ShapeDtypeStruct((nv, N_HEADS, bq, w), F32),
        compiler_params=_cparams(("arbitrary", "arbitrary")),
    )(rel_bias, idx)


def _window(i, nb, radius, length):
    w = QBLOCK + 2 * radius
    start = pl.multiple_of(jnp.clip(i * QBLOCK - radius, 0, length - w), radius)
    var = jnp.where(i == 0, 0, jnp.where(i == nb - 1, 2, 1))
    return start, var


def _qk(q, k):
    return lax.dot_general(q, k, (((1,), (1,)), ((), ())), preferred_element_type=F32)


def _attn_a_kernel(q_ref, k_ref, v_ref, bias_ref, o_ref, lse_ref, *, radius, length, nb):
    start, var = _window(pl.program_id(2), nb, radius, length)
    w = QBLOCK + 2 * radius
    for h in range(N_HEADS):
        cols = slice(h * HEAD_DIM, (h + 1) * HEAD_DIM)
        s = _qk(q_ref[:, cols], k_ref[pl.ds(start, w), cols]) + bias_ref[var, h]
        m = jnp.max(s, axis=-1, keepdims=True)
        p = jnp.exp(s - m)
        den = jnp.sum(p, axis=-1, keepdims=True)
        o = jnp.dot(p.astype(BF16), v_ref[pl.ds(start, w), cols], preferred_element_type=F32)
        o_ref[:, cols] = o / den
        lse_ref[:, h * LSE_LANES:(h + 1) * LSE_LANES] = jnp.broadcast_to(
            m + jnp.log(den), (QBLOCK, LSE_LANES))


def _attn_a(q, k, v, bias, *, window, dilation):
    B, S, _ = q.shape
    d = dilation
    L = S // d
    nb = L // QBLOCK
    radius = window // (2 * d)
    view = lambda t: t.reshape(B, L, d * t.shape[-1])
    o, lse = pl.pallas_call(
        functools.partial(_attn_a_kernel, radius=radius, length=L, nb=nb),
        grid=(B, d, nb),
        in_specs=[
            pl.BlockSpec((None, QBLOCK, MIX_A), lambda b, r, i: (b, i, r)),
            pl.BlockSpec((None, L, MIX_A), lambda b, r, i: (b, 0, r)),
            pl.BlockSpec((None, L, MIX_A), lambda b, r, i: (b, 0, r)),
            pl.BlockSpec(bias.shape, lambda b, r, i: (0, 0, 0, 0)),
        ],
        out_specs=[
            pl.BlockSpec((None, QBLOCK, MIX_A), lambda b, r, i: (b, i, r)),
            pl.BlockSpec((None, QBLOCK, LANE), lambda b, r, i: (b, i, r)),
        ],
        out_shape=[
            jax.ShapeDtypeStruct((B, L, d * MIX_A), F32),
            jax.ShapeDtypeStruct((B, L, d * LANE), F32),
        ],
        compiler_params=_cparams(("parallel", "parallel", "arbitrary")),
    )(view(q), view(k), view(v), bias)
    return o.reshape(B * S, MIX_A), lse.reshape(B * S, LANE)


def _attn_b_kernel(sink_ref, q_ref, k_ref, v_ref, bias_ref, o_ref, *, radius, length, nb):
    start, var = _window(pl.program_id(1), nb, radius, length)
    w = QBLOCK + 2 * radius
    for g in range(N_KV_B):
        kv_cols = slice(g * HEAD_DIM, (g + 1) * HEAD_DIM)
        k = k_ref[pl.ds(start, w), kv_cols]
        v = v_ref[pl.ds(start, w), kv_cols]
        for j in range(GROUP_B):
            h = g * GROUP_B + j
            cols = slice(h * HEAD_DIM, (h + 1) * HEAD_DIM)
            sink = sink_ref[h]
            s = _qk(q_ref[:, cols], k) + bias_ref[var, h]
            m = jnp.maximum(jnp.max(s, axis=-1, keepdims=True), sink)
            p = jnp.exp(s - m)
            den = jnp.sum(p, axis=-1, keepdims=True) + jnp.exp(sink - m)
            o = jnp.dot(p.astype(BF16), v, preferred_element_type=F32)
            o_ref[:, cols] = (o / den).astype(BF16)


def _attn_b(q, k, v, bias, sink):
    B, S, _ = q.shape
    nb = S // QBLOCK
    o = pl.pallas_call(
        functools.partial(_attn_b_kernel, radius=SWA_RADIUS, length=S, nb=nb),
        grid=(B, nb),
        in_specs=[
            pl.BlockSpec(memory_space=pltpu.SMEM),
            pl.BlockSpec((None, QBLOCK, MIX_A), lambda b, i: (b, i, 0)),
            pl.BlockSpec((None, S, KV_B), lambda b, i: (b, 0, 0)),
            pl.BlockSpec((None, S, KV_B), lambda b, i: (b, 0, 0)),
            pl.BlockSpec(bias.shape, lambda b, i: (0, 0, 0, 0)),
        ],
        out_specs=pl.BlockSpec((None, QBLOCK, MIX_A), lambda b, i: (b, i, 0)),
        out_shape=jax.ShapeDtypeStruct((B, S, MIX_A), BF16),
        compiler_params=_cparams(("parallel", "arbitrary")),
    )(sink, q, k, v, bias)
    return o.reshape(B * S, MIX_A)


def _wo_kernel(x_ref, o1_ref, o2_ref, o3_ref, l1_ref, l2_ref, l3_ref, ob_ref, w_ref, out_ref):
    l1, l2, l3 = l1_ref[...], l2_ref[...], l3_ref[...]
    mx = jnp.maximum(jnp.maximum(l1, l2), l3)
    e1, e2, e3 = jnp.exp(l1 - mx), jnp.exp(l2 - mx), jnp.exp(l3 - mx)
    tot = e1 + e2 + e3
    wts = (e1 / tot, e2 / tot, e3 / tot)
    tm = x_ref.shape[0]
    lane = lax.broadcasted_iota(jnp.int32, (tm, LANE), 1)
    acc = x_ref[...] + jnp.dot(ob_ref[...], w_ref[MIX_A:, :], preferred_element_type=F32)
    for c in range(MIX_A // LANE):
        lo, hi = 2 * c * LSE_LANES, (2 * c + 1) * LSE_LANES
        merged = jnp.zeros((tm, LANE), F32)
        for wt, o_ref in zip(wts, (o1_ref, o2_ref, o3_ref)):
            wfull = jnp.where(lane < HEAD_DIM, wt[:, lo:lo + 1], wt[:, hi:hi + 1])
            merged = merged + wfull * o_ref[:, c * LANE:(c + 1) * LANE]
        acc = acc + jnp.dot(merged.astype(BF16), w_ref[c * LANE:(c + 1) * LANE, :],
                            preferred_element_type=F32)
    out_ref[...] = acc


def _wo(x, outs, lses, ob, w, *, tm=512):
    T, D = x.shape
    row = lambda width: pl.BlockSpec((tm, width), lambda i: (i, 0))
    return pl.pallas_call(
        _wo_kernel,
        grid=(T // tm,),
        in_specs=[row(D)] + [row(MIX_A)] * 3 + [row(LANE)] * 3 + [row(MIX_A),
                  pl.BlockSpec(w.shape, lambda i: (0, 0))],
        out_specs=row(D),
        out_shape=jax.ShapeDtypeStruct((T, D), F32),
        compiler_params=_cparams(("parallel",)),
    )(x, *outs, *lses, ob, w)


def _ple_kernel(x_ref, p_ref, g_ref, wg_ref, wp_ref, o_ref):
    x = x_ref[...]
    h = (x * _rms_scale(x) * g_ref[...]).astype(BF16)
    gate = jax.nn.sigmoid(jnp.dot(h, wg_ref[...], preferred_element_type=F32))
    proj = jnp.dot(p_ref[...].astype(BF16), wp_ref[...], preferred_element_type=F32)
    o_ref[...] = x + gate * proj


def _ple(x, p, g, w_gate, w_proj, *, tm=512):
    T, D = x.shape
    return pl.pallas_call(
        _ple_kernel,
        grid=(T // tm,),
        in_specs=[
            pl.BlockSpec((tm, D), lambda i: (i, 0)),
            pl.BlockSpec((tm, p.shape[1]), lambda i: (i, 0)),
            pl.BlockSpec((1, D), lambda i: (0, 0)),
            pl.BlockSpec(w_gate.shape, lambda i: (0, 0)),
            pl.BlockSpec(w_proj.shape, lambda i: (0, 0)),
        ],
        out_specs=pl.BlockSpec((tm, D), lambda i: (i, 0)),
        out_shape=jax.ShapeDtypeStruct((T, D), F32),
        compiler_params=_cparams(("parallel",)),
    )(x, p, g, w_gate, w_proj)


def kernel(x, p, rel_bias, norm_ffn1, ffn1_w_in, ffn1_w_out, norm_mix, w_qkv, q_norm_a, k_norm_a, q_norm_b, k_norm_b, sink_b, w_o, norm_ffn2, ffn2_w_in, ffn2_w_out, norm_ple, w_ple_gate, w_ple_proj):
    B, S, D = x.shape
    depth = p.shape[0]
    T = B * S
    x = x.reshape(T, D)
    p = p.reshape(depth, T, p.shape[-1])

    bias_a = [_bias_tiles(rel_bias, window // (2 * d), d, 0) for window, d in DILATED_CONFIGS]
    bias_b = _bias_tiles(rel_bias, SWA_RADIUS, 1, N_HEADS)

    scale = HEAD_DIM ** -0.5
    tile = lambda g, n: jnp.tile(g, n)
    for i in range(depth):
        x = _ffn(x, norm_ffn1[i][None], ffn1_w_in[i].astype(BF16), ffn1_w_out[i].astype(BF16))

        gains = jnp.concatenate([
            tile(q_norm_a[i] * scale, N_HEADS), tile(k_norm_a[i], N_HEADS),
            jnp.ones((MIX_A,), F32),
            tile(q_norm_b[i] * scale, N_HEADS), tile(k_norm_b[i], N_KV_B),
            jnp.ones((KV_B,), F32)])[None]
        qa, ka, va, qb, kb, vb = _qkv(x, norm_mix[i][None], w_qkv[i].astype(BF16), gains)
        seq = lambda t: t.reshape(B, S, t.shape[-1])

        outs, lses = [], []
        for (window, d), bias in zip(DILATED_CONFIGS, bias_a):
            o, lse = _attn_a(seq(qa), seq(ka), seq(va), bias, window=window, dilation=d)
            outs.append(o)
            lses.append(lse)
        ob = _attn_b(seq(qb), seq(kb), seq(vb), bias_b, sink_b[i])

        x = _wo(x, outs, lses, ob, w_o[i].astype(BF16))
        x = _ffn(x, norm_ffn2[i][None], ffn2_w_in[i].astype(BF16), ffn2_w_out[i].astype(BF16))
        x = _ple(x, p[i], norm_ple[i][None], w_ple_gate[i].astype(BF16), w_ple_proj[i].astype(BF16))
    return x.reshape(B, S, D)
```

```python
import functools
import math

import jax
import jax.numpy as jnp
import numpy as np
from jax import lax
from jax.experimental import pallas as pl
from jax.experimental.pallas import tpu as pltpu

HEAD_DIM = 64
N_HEADS = 8
N_KV_B = 2
GROUP_B = N_HEADS // N_KV_B
MIX_A = N_HEADS * HEAD_DIM
KV_B = N_KV_B * HEAD_DIM
DILATED_CONFIGS = ((128, 1), (512, 4), (2048, 16))
SWA_RADIUS = 128
N_BUCKETS = 32
MAX_DISTANCE = 1024
EPS = 1e-6
NEG = -1e30
QBLOCK = 128
ATTN_GROUP = 4
LSE_LANES = 16
LANE = 128

VMEM_LIMIT = 48 * 1024 * 1024

BF16 = jnp.bfloat16
F32 = jnp.float32


def _cparams(sem):
    return pltpu.CompilerParams(dimension_semantics=sem, vmem_limit_bytes=VMEM_LIMIT)


def _rms_scale(x):
    return lax.rsqrt(jnp.mean(x * x, axis=-1, keepdims=True) + EPS)


def _ffn_kernel(x_ref, g_ref, wg_ref, wu_ref, wo_ref, o_ref, h_ref, acc_ref, *, nf):
    j = pl.program_id(1)

    @pl.when(j == 0)
    def _():
        x = x_ref[...]
        h_ref[...] = (x * _rms_scale(x) * g_ref[...]).astype(BF16)
        acc_ref[...] = jnp.zeros_like(acc_ref)

    h = h_ref[...]
    gate = jnp.dot(h, wg_ref[...], preferred_element_type=F32)
    up = jnp.dot(h, wu_ref[...], preferred_element_type=F32)
    act = (gate * jax.nn.sigmoid(gate) * up).astype(BF16)
    acc_ref[...] += jnp.dot(act, wo_ref[...], preferred_element_type=F32)

    @pl.when(j == nf - 1)
    def _():
        o_ref[...] = x_ref[...] + 0.5 * acc_ref[...]


def _ffn(x, g, w_in, w_out, *, tm=1024, tf=256):
    T, D = x.shape
    d_ff = w_out.shape[0]
    nf = d_ff // tf
    return pl.pallas_call(
        functools.partial(_ffn_kernel, nf=nf),
        grid=(T // tm, nf),
        in_specs=[
            pl.BlockSpec((tm, D), lambda i, j: (i, 0)),
            pl.BlockSpec((1, D), lambda i, j: (0, 0)),
            pl.BlockSpec((D, tf), lambda i, j: (0, j)),
            pl.BlockSpec((D, tf), lambda i, j: (0, j + nf)),
            pl.BlockSpec((tf, D), lambda i, j: (j, 0)),
        ],
        out_specs=pl.BlockSpec((tm, D), lambda i, j: (i, 0)),
        out_shape=jax.ShapeDtypeStruct((T, D), F32),
        scratch_shapes=[pltpu.VMEM((tm, D), BF16), pltpu.VMEM((tm, D), F32)],
        compiler_params=_cparams(("parallel", "arbitrary")),
        name="ffn",
    )(x, g, w_in, w_in, w_out)


def _head_mean_sq(y):
    sq = y * y
    hi = sq.astype(BF16)
    lo = (sq - hi.astype(F32)).astype(BF16)
    r = lax.broadcasted_iota(jnp.int32, (LANE, LANE), 0) // HEAD_DIM
    c = lax.broadcasted_iota(jnp.int32, (LANE, LANE), 1) // HEAD_DIM
    seg = jnp.where(r == c, 1.0 / HEAD_DIM, 0.0).astype(BF16)
    return (jnp.dot(hi, seg, preferred_element_type=F32)
            + jnp.dot(lo, seg, preferred_element_type=F32))


def _qkv_kernel(x_ref, g_ref, w_ref, gq_ref, qa_ref, ka_ref, va_ref, qb_ref, kb_ref, vb_ref):
    x = x_ref[...]
    h = (x * _rms_scale(x) * g_ref[...]).astype(BF16)
    col = 0
    for ref, normed in ((qa_ref, True), (ka_ref, True), (va_ref, False),
                        (qb_ref, True), (kb_ref, True), (vb_ref, False)):
        width = ref.shape[-1]
        y = jnp.dot(h, w_ref[:, col:col + width], preferred_element_type=F32)
        if normed:
            for c in range(0, width, LANE):
                yc = y[:, c:c + LANE]
                gain = gq_ref[:, col + c:col + c + LANE]
                ref[:, c:c + LANE] = (yc * lax.rsqrt(_head_mean_sq(yc) + EPS) * gain).astype(BF16)
        else:
            ref[...] = y.astype(BF16)
        col += width


def _qkv(x, g, w, gains, *, tm=512):
    T, D = x.shape
    widths = (MIX_A, MIX_A, MIX_A, MIX_A, KV_B, KV_B)
    return pl.pallas_call(
        _qkv_kernel,
        grid=(T // tm,),
        in_specs=[
            pl.BlockSpec((tm, D), lambda i: (i, 0)),
            pl.BlockSpec((1, D), lambda i: (0, 0)),
            pl.BlockSpec(w.shape, lambda i: (0, 0)),
            pl.BlockSpec(gains.shape, lambda i: (0, 0)),
        ],
        out_specs=[pl.BlockSpec((tm, wd), lambda i: (i, 0)) for wd in widths],
        out_shape=[jax.ShapeDtypeStruct((T, wd), BF16) for wd in widths],
        compiler_params=_cparams(("parallel",)),
        name="qkv",
    )(x, g, w, gains)


def _t5_bucket_np(rel):
    half = N_BUCKETS // 2
    max_exact = half // 2
    ret = np.where(rel > 0, half, 0)
    n = np.abs(rel)
    nf = np.maximum(n, 1).astype(np.float64)
    large = max_exact + (np.log(nf / max_exact) / math.log(MAX_DISTANCE / max_exact)
                         * (half - max_exact)).astype(np.int64)
    large = np.minimum(large, half - 1)
    return ret + np.where(n < max_exact, n, large)


def _bucket_tiles(radius, dilation):
    w = QBLOCK + 2 * radius
    key = np.arange(w)[:, None]
    q = np.arange(QBLOCK)[None, :]
    tiles = []
    for off in (0, -radius, -2 * radius):
        rel = off + key - q
        tiles.append(np.where(np.abs(rel) <= radius, _t5_bucket_np(rel * dilation), -1))
    return np.stack(tiles).astype(np.int32)


def _bias_kernel(rb_ref, idx_ref, o_ref, *, head0):
    h = pl.program_id(1) + head0
    idx = idx_ref[...]
    tile = jnp.full(idx.shape, NEG, F32)
    for b in range(N_BUCKETS):
        tile = jnp.where(idx == b, rb_ref[b, h], tile)
    o_ref[...] = tile


def _bias_tiles(rel_bias, radius, dilation, head0):
    idx = jnp.asarray(_bucket_tiles(radius, dilation))
    nv, w, bq = idx.shape
    return pl.pallas_call(
        functools.partial(_bias_kernel, head0=head0),
        grid=(nv, N_HEADS),
        in_specs=[
            pl.BlockSpec(memory_space=pltpu.SMEM),
            pl.BlockSpec((None, w, bq), lambda v, h: (v, 0, 0)),
        ],
        out_specs=pl.BlockSpec((None, None, w, bq), lambda v, h: (v, h, 0, 0)),
        out_shape=jax.ShapeDtypeStruct((nv, N_HEADS, w, bq), F32),
        compiler_params=_cparams(("arbitrary", "arbitrary")),
        name="bias_tiles",
    )(rel_bias, idx)


def _window(i, nb, radius, length):
    w = QBLOCK + 2 * radius
    start = pl.multiple_of(jnp.clip(i * QBLOCK - radius, 0, length - w), radius)
    var = jnp.where(i == 0, 0, jnp.where(i == nb - 1, 2, 1))
    return start, var


_NT = (((1,), (1,)), ((), ()))
_TN = (((0,), (0,)), ((), ()))


def _head_softmax(kp, qh, vp, bias, sink=None):
    s = lax.dot_general(kp, qh, _NT, preferred_element_type=F32) + bias
    m = jnp.max(s, axis=0, keepdims=True)
    if sink is not None:
        m = jnp.maximum(m, sink)
    p = jnp.exp(s - m)
    den = jnp.sum(p, axis=0, keepdims=True)
    if sink is not None:
        den = den + jnp.exp(sink - m)
    o = lax.dot_general(vp, p.astype(BF16), _TN, preferred_element_type=F32)
    return o * (1.0 / den), m + jnp.log(den)


def _pair_attention(qp, kp, vp, bias_lo, bias_hi, sink_lo=None, sink_hi=None):
    lane = lax.broadcasted_iota(jnp.int32, qp.shape, 1)
    zero = jnp.zeros_like(qp)
    o_lo, lse_lo = _head_softmax(kp, jnp.where(lane < HEAD_DIM, qp, zero), vp, bias_lo, sink_lo)
    o_hi, lse_hi = _head_softmax(kp, jnp.where(lane >= HEAD_DIM, qp, zero), vp, bias_hi, sink_hi)
    o_t = jnp.concatenate([o_lo[:HEAD_DIM], o_hi[HEAD_DIM:]], axis=0)
    return o_t.T, lse_lo, lse_hi


def _attn_a_kernel(q_ref, k_ref, v_ref, bias_ref, o_ref, lse_ref, *, radius, length, nb, group):
    w = QBLOCK + 2 * radius
    for g in range(group):
        start, var = _window(pl.program_id(2) * group + g, nb, radius, length)
        rows = slice(g * QBLOCK, (g + 1) * QBLOCK)
        lses = []
        for pr in range(N_HEADS // 2):
            cols = slice(pr * LANE, (pr + 1) * LANE)
            o, lse_lo, lse_hi = _pair_attention(
                q_ref[rows, cols], k_ref[pl.ds(start, w), cols], v_ref[pl.ds(start, w), cols],
                bias_ref[var, 2 * pr], bias_ref[var, 2 * pr + 1])
            o_ref[rows, cols] = o
            lses += [lse_lo, lse_hi]
        lse_t = jnp.concatenate([jnp.broadcast_to(l, (LSE_LANES, QBLOCK)) for l in lses], axis=0)
        lse_ref[rows, :] = lse_t.T


def _attn_a(q, k, v, bias, *, window, dilation):
    B, S, _ = q.shape
    d = dilation
    L = S // d
    nb = L // QBLOCK
    radius = window // (2 * d)
    group = min(nb, ATTN_GROUP)
    view = lambda t: t.reshape(B, L, d * t.shape[-1])
    o, lse = pl.pallas_call(
        functools.partial(_attn_a_kernel, radius=radius, length=L, nb=nb, group=group),
        grid=(B, d, nb // group),
        in_specs=[
            pl.BlockSpec((None, group * QBLOCK, MIX_A), lambda b, r, i: (b, i, r)),
            pl.BlockSpec((None, L, MIX_A), lambda b, r, i: (b, 0, r)),
            pl.BlockSpec((None, L, MIX_A), lambda b, r, i: (b, 0, r)),
            pl.BlockSpec(bias.shape, lambda b, r, i: (0, 0, 0, 0)),
        ],
        out_specs=[
            pl.BlockSpec((None, group * QBLOCK, MIX_A), lambda b, r, i: (b, i, r)),
            pl.BlockSpec((None, group * QBLOCK, LANE), lambda b, r, i: (b, i, r)),
        ],
        out_shape=[
            jax.ShapeDtypeStruct((B, L, d * MIX_A), F32),
            jax.ShapeDtypeStruct((B, L, d * LANE), F32),
        ],
        compiler_params=_cparams(("parallel", "parallel", "arbitrary")),
        name=f"attn_a_d{d}",
    )(view(q), view(k), view(v), bias)
    return o.reshape(B * S, MIX_A), lse.reshape(B * S, LANE)


def _attn_b_kernel(sink_ref, q_ref, k_ref, v_ref, bias_ref, o_ref, *, radius, length, nb, group):
    w = QBLOCK + 2 * radius
    for g in range(group):
        start, var = _window(pl.program_id(1) * group + g, nb, radius, length)
        rows = slice(g * QBLOCK, (g + 1) * QBLOCK)
        k = k_ref[pl.ds(start, w), :]
        v = v_ref[pl.ds(start, w), :]
        for pr in range(GROUP_B):
            cols = slice(pr * LANE, (pr + 1) * LANE)
            lo, hi = pr, pr + GROUP_B
            o, _, _ = _pair_attention(q_ref[rows, cols], k, v, bias_ref[var, lo],
                                      bias_ref[var, hi], sink_ref[lo], sink_ref[hi])
            o_ref[rows, cols] = o.astype(BF16)


def _attn_b(q, k, v, bias, sink):
    B, S, _ = q.shape
    nb = S // QBLOCK
    group = min(nb, ATTN_GROUP)
    o = pl.pallas_call(
        functools.partial(_attn_b_kernel, radius=SWA_RADIUS, length=S, nb=nb, group=group),
        grid=(B, nb // group),
        in_specs=[
            pl.BlockSpec(memory_space=pltpu.SMEM),
            pl.BlockSpec((None, group * QBLOCK, MIX_A), lambda b, i: (b, i, 0)),
            pl.BlockSpec((None, S, KV_B), lambda b, i: (b, 0, 0)),
            pl.BlockSpec((None, S, KV_B), lambda b, i: (b, 0, 0)),
            pl.BlockSpec(bias.shape, lambda b, i: (0, 0, 0, 0)),
        ],
        out_specs=pl.BlockSpec((None, group * QBLOCK, MIX_A), lambda b, i: (b, i, 0)),
        out_shape=jax.ShapeDtypeStruct((B, S, MIX_A), BF16),
        compiler_params=_cparams(("parallel", "arbitrary")),
        name="attn_b",
    )(sink, q, k, v, bias)
    return o.reshape(B * S, MIX_A)


def _wo_kernel(x_ref, o1_ref, o2_ref, o3_ref, l1_ref, l2_ref, l3_ref, ob_ref, w_ref, out_ref):
    l1, l2, l3 = l1_ref[...], l2_ref[...], l3_ref[...]
    mx = jnp.maximum(jnp.maximum(l1, l2), l3)
    e1, e2, e3 = jnp.exp(l1 - mx), jnp.exp(l2 - mx), jnp.exp(l3 - mx)
    tot = e1 + e2 + e3
    wts = (e1 / tot, e2 / tot, e3 / tot)
    tm = x_ref.shape[0]
    lane = lax.broadcasted_iota(jnp.int32, (tm, LANE), 1)
    acc = x_ref[...] + jnp.dot(ob_ref[...], w_ref[MIX_A:, :], preferred_element_type=F32)
    for c in range(MIX_A // LANE):
        lo, hi = 2 * c * LSE_LANES, (2 * c + 1) * LSE_LANES
        merged = jnp.zeros((tm, LANE), F32)
        for wt, o_ref in zip(wts, (o1_ref, o2_ref, o3_ref)):
            wfull = jnp.where(lane < HEAD_DIM, wt[:, lo:lo + 1], wt[:, hi:hi + 1])
            merged = merged + wfull * o_ref[:, c * LANE:(c + 1) * LANE]
        acc = acc + jnp.dot(merged.astype(BF16), w_ref[c * LANE:(c + 1) * LANE, :],
                            preferred_element_type=F32)
    out_ref[...] = acc


def _wo(x, outs, lses, ob, w, *, tm=512):
    T, D = x.shape
    row = lambda width: pl.BlockSpec((tm, width), lambda i: (i, 0))
    return pl.pallas_call(
        _wo_kernel,
        grid=(T // tm,),
        in_specs=[row(D)] + [row(MIX_A)] * 3 + [row(LANE)] * 3 + [row(MIX_A),
                  pl.BlockSpec(w.shape, lambda i: (0, 0))],
        out_specs=row(D),
        out_shape=jax.ShapeDtypeStruct((T, D), F32),
        compiler_params=_cparams(("parallel",)),
        name="merge_wo",
    )(x, *outs, *lses, ob, w)


def _ple_kernel(x_ref, p_ref, g_ref, wg_ref, wp_ref, o_ref):
    x = x_ref[...]
    h = (x * _rms_scale(x) * g_ref[...]).astype(BF16)
    gate = jax.nn.sigmoid(jnp.dot(h, wg_ref[...], preferred_element_type=F32))
    proj = jnp.dot(p_ref[...].astype(BF16), wp_ref[...], preferred_element_type=F32)
    o_ref[...] = x + gate * proj


def _ple(x, p, g, w_gate, w_proj, *, tm=512):
    T, D = x.shape
    return pl.pallas_call(
        _ple_kernel,
        grid=(T // tm,),
        in_specs=[
            pl.BlockSpec((tm, D), lambda i: (i, 0)),
            pl.BlockSpec((tm, p.shape[1]), lambda i: (i, 0)),
            pl.BlockSpec((1, D), lambda i: (0, 0)),
            pl.BlockSpec(w_gate.shape, lambda i: (0, 0)),
            pl.BlockSpec(w_proj.shape, lambda i: (0, 0)),
        ],
        out_specs=pl.BlockSpec((tm, D), lambda i: (i, 0)),
        out_shape=jax.ShapeDtypeStruct((T, D), F32),
        compiler_params=_cparams(("parallel",)),
        name="ple",
    )(x, p, g, w_gate, w_proj)


def kernel(x, p, rel_bias, norm_ffn1, ffn1_w_in, ffn1_w_out, norm_mix, w_qkv, q_norm_a, k_norm_a, q_norm_b, k_norm_b, sink_b, w_o, norm_ffn2, ffn2_w_in, ffn2_w_out, norm_ple, w_ple_gate, w_ple_proj):
    B, S, D = x.shape
    depth = p.shape[0]
    T = B * S
    x = x.reshape(T, D)
    p = p.reshape(depth, T, p.shape[-1])

    bias_a = [_bias_tiles(rel_bias, window // (2 * d), d, 0) for window, d in DILATED_CONFIGS]
    bias_b = _bias_tiles(rel_bias, SWA_RADIUS, 1, N_HEADS)

    scale = HEAD_DIM ** -0.5
    tile = lambda g, n: jnp.tile(g, n)
    head_order = np.arange(N_HEADS).reshape(N_KV_B, GROUP_B).T.reshape(-1)
    b_cols = (head_order[:, None] * HEAD_DIM + np.arange(HEAD_DIM)[None, :]).reshape(-1)
    qkv_cols = np.arange(w_qkv.shape[-1])
    qkv_cols[3 * MIX_A:4 * MIX_A] = 3 * MIX_A + b_cols
    wo_rows = np.concatenate([np.arange(MIX_A), MIX_A + b_cols])
    for i in range(depth):
        x = _ffn(x, norm_ffn1[i][None], ffn1_w_in[i].astype(BF16), ffn1_w_out[i].astype(BF16))

        gains = jnp.concatenate([
            tile(q_norm_a[i] * scale, N_HEADS), tile(k_norm_a[i], N_HEADS),
            jnp.ones((MIX_A,), F32),
            tile(q_norm_b[i] * scale, N_HEADS), tile(k_norm_b[i], N_KV_B),
            jnp.ones((KV_B,), F32)])[None]
        qa, ka, va, qb, kb, vb = _qkv(x, norm_mix[i][None],
                                      w_qkv[i][:, qkv_cols].astype(BF16), gains)
        seq = lambda t: t.reshape(B, S, t.shape[-1])

        outs, lses = [], []
        for (window, d), bias in zip(DILATED_CONFIGS, bias_a):
            o, lse = _attn_a(seq(qa), seq(ka), seq(va), bias, window=window, dilation=d)
            outs.append(o)
            lses.append(lse)
        ob = _attn_b(seq(qb), seq(kb), seq(vb), bias_b, sink_b[i])

        x = _wo(x, outs, lses, ob, w_o[i][wo_rows].astype(BF16))
        x = _ffn(x, norm_ffn2[i][None], ffn2_w_in[i].astype(BF16), ffn2_w_out[i].astype(BF16))
        x = _ple(x, p[i], norm_ple[i][None], w_ple_gate[i].astype(BF16), w_ple_proj[i].astype(BF16))
    return x.reshape(B, S, D)
```

```python
import functools
import math

import jax
import jax.numpy as jnp
import numpy as np
from jax import lax
from jax.experimental import pallas as pl
from jax.experimental.pallas import tpu as pltpu

HEAD_DIM = 64
N_HEADS = 8
N_KV_B = 2
GROUP_B = N_HEADS // N_KV_B
MIX_A = N_HEADS * HEAD_DIM
KV_B = N_KV_B * HEAD_DIM
DILATED_CONFIGS = ((128, 1), (512, 4), (2048, 16))
DILATIONS = tuple(d for _, d in DILATED_CONFIGS)
SWA_RADIUS = 128
N_BUCKETS = 32
MAX_DISTANCE = 1024
EPS = 1e-6
NEG = -1e30
QBLOCK = 128
ATTN_GROUP = 4
LSE_LANES = 16
LANE = 128
FF_CHUNK = 256
ROW_TILE = 512

VMEM_LIMIT = 56 * 1024 * 1024

BF16 = jnp.bfloat16
F32 = jnp.float32


def _cparams(sem):
    return pltpu.CompilerParams(dimension_semantics=sem, vmem_limit_bytes=VMEM_LIMIT)


def _resident(shape, index_map):
    return pl.BlockSpec(shape, index_map, pipeline_mode=pl.Buffered(1))


def _layer_block(w, layer):
    return _resident((None,) + w.shape[1:], lambda i: (layer, 0, 0))


def _rms_scale(x):
    return lax.rsqrt(jnp.mean(x * x, axis=-1, keepdims=True) + EPS)


def _ffn_kernel(x_ref, g_ref, win_ref, wout_ref, o_ref, act_ref, *, d_ff):
    x = x_ref[...]
    h = (x * _rms_scale(x) * g_ref[...]).astype(BF16)
    for c in range(0, d_ff, FF_CHUNK):
        gate = jnp.dot(h, win_ref[:, c:c + FF_CHUNK], preferred_element_type=F32)
        up = jnp.dot(h, win_ref[:, d_ff + c:d_ff + c + FF_CHUNK], preferred_element_type=F32)
        act_ref[:, c:c + FF_CHUNK] = (gate * jax.nn.sigmoid(gate) * up).astype(BF16)
    o_ref[...] = x_ref[...] + 0.5 * jnp.dot(act_ref[...], wout_ref[...],
                                             preferred_element_type=F32)


def _ffn(x, g, w_in, w_out, layer, *, tm=ROW_TILE):
    T, D = x.shape
    d_ff = w_out.shape[1]
    return pl.pallas_call(
        functools.partial(_ffn_kernel, d_ff=d_ff),
        grid=(T // tm,),
        in_specs=[
            pl.BlockSpec((tm, D), lambda i: (i, 0)),
            _resident((None, 1, D), lambda i: (layer, 0, 0)),
            _layer_block(w_in, layer),
            _layer_block(w_out, layer),
        ],
        out_specs=pl.BlockSpec((tm, D), lambda i: (i, 0)),
        out_shape=jax.ShapeDtypeStruct((T, D), F32),
        scratch_shapes=[pltpu.VMEM((tm, d_ff), BF16)],
        compiler_params=_cparams(("parallel",)),
        name="ffn",
    )(x, g, w_in, w_out)


def _head_mean_sq(y):
    sq = y * y
    hi = sq.astype(BF16)
    lo = (sq - hi.astype(F32)).astype(BF16)
    r = lax.broadcasted_iota(jnp.int32, (LANE, LANE), 0) // HEAD_DIM
    c = lax.broadcasted_iota(jnp.int32, (LANE, LANE), 1) // HEAD_DIM
    seg = jnp.where(r == c, 1.0 / HEAD_DIM, 0.0).astype(BF16)
    return (jnp.dot(hi, seg, preferred_element_type=F32)
            + jnp.dot(lo, seg, preferred_element_type=F32))


def _qk_normed(y, gain):
    return y * lax.rsqrt(_head_mean_sq(y) + EPS) * gain


def _qkv_kernel(x_ref, g_ref, w_ref, gq_ref, *refs):
    n_d = len(DILATIONS)
    a_refs, (qb_ref, kb_ref, vb_ref, scr_ref) = refs[:3 * n_d], refs[3 * n_d:]
    x = x_ref[...]
    tm = x.shape[0]
    h = (x * _rms_scale(x) * g_ref[...]).astype(BF16)
    n_blk = MIX_A // LANE
    col = 0
    for gi, normed in enumerate((True, True, False)):
        y = jnp.dot(h, w_ref[:, col:col + MIX_A], preferred_element_type=F32)
        for c in range(n_blk):
            yc = y[:, c * LANE:(c + 1) * LANE]
            if normed:
                yc = _qk_normed(yc, gq_ref[:, col + c * LANE:col + (c + 1) * LANE])
            scr_ref[gi * n_blk + c] = yc
        for di, d in enumerate(DILATIONS):
            out = a_refs[gi * n_d + di]
            for r in range(d):
                for c in range(n_blk):
                    slab = scr_ref.at[gi * n_blk + c]
                    rows = slab[...] if d == 1 else slab[pl.ds(r, tm // d, stride=d), :]
                    out[:, r * MIX_A + c * LANE:r * MIX_A + (c + 1) * LANE] = rows.astype(BF16)
        col += MIX_A
    yq = jnp.dot(h, w_ref[:, col:col + MIX_A], preferred_element_type=F32)
    for c in range(n_blk):
        cols = slice(c * LANE, (c + 1) * LANE)
        qb_ref[:, cols] = _qk_normed(yq[:, cols], gq_ref[:, col + c * LANE:col + (c + 1) * LANE]
                                     ).astype(BF16)
    col += MIX_A
    ykv = jnp.dot(h, w_ref[:, col:col + 2 * KV_B], preferred_element_type=F32)
    kb_ref[...] = _qk_normed(ykv[:, :KV_B], gq_ref[:, col:col + KV_B]).astype(BF16)
    vb_ref[...] = ykv[:, KV_B:].astype(BF16)


def _qkv(x, g, w, gains, layer, *, tm=ROW_TILE):
    T, D = x.shape
    a_shapes = [(d, MIX_A) for _ in range(3) for d in DILATIONS]
    shapes = [(T // d, d * c) for d, c in a_shapes] + [(T, MIX_A), (T, KV_B), (T, KV_B)]
    blocks = [(tm // d, d * c) for d, c in a_shapes] + [(tm, MIX_A), (tm, KV_B), (tm, KV_B)]
    outs = pl.pallas_call(
        _qkv_kernel,
        grid=(T // tm,),
        in_specs=[
            pl.BlockSpec((tm, D), lambda i: (i, 0)),
            _resident((None, 1, D), lambda i: (layer, 0, 0)),
            _layer_block(w, layer),
            _resident((None, 1, gains.shape[-1]), lambda i: (layer, 0, 0)),
        ],
        out_specs=[pl.BlockSpec(b, lambda i: (i, 0)) for b in blocks],
        out_shape=[jax.ShapeDtypeStruct(s, BF16) for s in shapes],
        scratch_shapes=[pltpu.VMEM((3 * MIX_A // LANE, tm, LANE), F32)],
        compiler_params=_cparams(("parallel",)),
        name="qkv",
    )(x, g, w, gains)
    n_d = len(DILATIONS)
    return (outs[:n_d], outs[n_d:2 * n_d], outs[2 * n_d:3 * n_d]), outs[3 * n_d:]


def _t5_bucket_np(rel):
    half = N_BUCKETS // 2
    max_exact = half // 2
    ret = np.where(rel > 0, half, 0)
    n = np.abs(rel)
    nf = np.maximum(n, 1).astype(np.float64)
    large = max_exact + (np.log(nf / max_exact) / math.log(MAX_DISTANCE / max_exact)
                         * (half - max_exact)).astype(np.int64)
    large = np.minimum(large, half - 1)
    return ret + np.where(n < max_exact, n, large)


def _bucket_tiles(radius, dilation):
    w = QBLOCK + 2 * radius
    key = np.arange(w)[:, None]
    q = np.arange(QBLOCK)[None, :]
    tiles = []
    for off in (0, -radius, -2 * radius):
        rel = off + key - q
        tiles.append(np.where(np.abs(rel) <= radius, _t5_bucket_np(rel * dilation), -1))
    return np.stack(tiles).astype(np.int32)


def _bias_kernel(rb_ref, idx_ref, o_ref, *, head0):
    h = pl.program_id(1) + head0
    idx = idx_ref[...]
    tile = jnp.full(idx.shape, NEG, F32)
    for b in range(N_BUCKETS):
        tile = jnp.where(idx == b, rb_ref[b, h], tile)
    o_ref[...] = tile


def _bias_tiles(rel_bias, radius, dilation, head0):
    idx = jnp.asarray(_bucket_tiles(radius, dilation))
    nv, w, bq = idx.shape
    return pl.pallas_call(
        functools.partial(_bias_kernel, head0=head0),
        grid=(nv, N_HEADS),
        in_specs=[
            pl.BlockSpec(memory_space=pltpu.SMEM),
            pl.BlockSpec((None, w, bq), lambda v, h: (v, 0, 0)),
        ],
        out_specs=pl.BlockSpec((None, None, w, bq), lambda v, h: (v, h, 0, 0)),
        out_shape=jax.ShapeDtypeStruct((nv, N_HEADS, w, bq), F32),
        compiler_params=_cparams(("arbitrary", "arbitrary")),
        name="bias_tiles",
    )(rel_bias, idx)


def _window(i, nb, radius, length):
    w = QBLOCK + 2 * radius
    start = pl.multiple_of(jnp.clip(i * QBLOCK - radius, 0, length - w), radius)
    var = jnp.where(i == 0, 0, jnp.where(i == nb - 1, 2, 1))
    return start, var


_NT = (((1,), (1,)), ((), ()))
_TN = (((0,), (0,)), ((), ()))


def _head_softmax(kp, qh, vp, bias, sink=None):
    s = lax.dot_general(kp, qh, _NT, preferred_element_type=F32) + bias
    m = jnp.max(s, axis=0, keepdims=True)
    if sink is not None:
        m = jnp.maximum(m, sink)
    p = jnp.exp(s - m)
    den = jnp.sum(p, axis=0, keepdims=True)
    if sink is not None:
        den = den + jnp.exp(sink - m)
    o = lax.dot_general(vp, p.astype(BF16), _TN, preferred_element_type=F32)
    return o * (1.0 / den), m + jnp.log(den)


def _pair_attention(qp, kp, vp, bias_lo, bias_hi, sink_lo=None, sink_hi=None):
    lane = lax.broadcasted_iota(jnp.int32, qp.shape, 1)
    zero = jnp.zeros_like(qp)
    o_lo, lse_lo = _head_softmax(kp, jnp.where(lane < HEAD_DIM, qp, zero), vp, bias_lo, sink_lo)
    o_hi, lse_hi = _head_softmax(kp, jnp.where(lane >= HEAD_DIM, qp, zero), vp, bias_hi, sink_hi)
    o_t = jnp.concatenate([o_lo[:HEAD_DIM], o_hi[HEAD_DIM:]], axis=0)
    return o_t.T, lse_lo, lse_hi


def _attn_a_kernel(q_ref, k_ref, v_ref, bias_ref, o_ref, lse_ref, *, radius, length, nb, group):
    w = QBLOCK + 2 * radius
    for g in range(group):
        start, var = _window(pl.program_id(2) * group + g, nb, radius, length)
        rows = slice(g * QBLOCK, (g + 1) * QBLOCK)
        lses = []
        for pr in range(N_HEADS // 2):
            cols = slice(pr * LANE, (pr + 1) * LANE)
            o, lse_lo, lse_hi = _pair_attention(
                q_ref[rows, cols], k_ref[pl.ds(start, w), cols], v_ref[pl.ds(start, w), cols],
                bias_ref[var, 2 * pr], bias_ref[var, 2 * pr + 1])
            o_ref[rows, cols] = o.astype(BF16)
            lses += [lse_lo, lse_hi]
        lse_t = jnp.concatenate([jnp.broadcast_to(l, (LSE_LANES, QBLOCK)) for l in lses], axis=0)
        lse_ref[rows, :] = lse_t.T


def _attn_a(q, k, v, bias, batch, *, window, dilation):
    d = dilation
    L = q.shape[0] // batch
    nb = L // QBLOCK
    radius = window // (2 * d)
    group = min(nb, ATTN_GROUP)
    seq = lambda t: t.reshape(batch, L, t.shape[-1])
    o, lse = pl.pallas_call(
        functools.partial(_attn_a_kernel, radius=radius, length=L, nb=nb, group=group),
        grid=(batch, d, nb // group),
        in_specs=[
            pl.BlockSpec((None, group * QBLOCK, MIX_A), lambda b, r, i: (b, i, r)),
            pl.BlockSpec((None, L, MIX_A), lambda b, r, i: (b, 0, r)),
            pl.BlockSpec((None, L, MIX_A), lambda b, r, i: (b, 0, r)),
            _resident(bias.shape, lambda b, r, i: (0, 0, 0, 0)),
        ],
        out_specs=[
            pl.BlockSpec((None, group * QBLOCK, MIX_A), lambda b, r, i: (b, i, r)),
            pl.BlockSpec((None, group * QBLOCK, LANE), lambda b, r, i: (b, i, r)),
        ],
        out_shape=[
            jax.ShapeDtypeStruct((batch, L, d * MIX_A), BF16),
            jax.ShapeDtypeStruct((batch, L, d * LANE), F32),
        ],
        compiler_params=_cparams(("parallel", "parallel", "arbitrary")),
        name=f"attn_a_d{d}",
    )(seq(q), seq(k), seq(v), bias)
    return o.reshape(batch * L, d * MIX_A), lse.reshape(batch * L, d * LANE)


def _attn_b_kernel(sink_ref, q_ref, k_ref, v_ref, bias_ref, o_ref, *, radius, length, nb, group):
    w = QBLOCK + 2 * radius
    for g in range(group):
        start, var = _window(pl.program_id(1) * group + g, nb, radius, length)
        rows = slice(g * QBLOCK, (g + 1) * QBLOCK)
        k = k_ref[pl.ds(start, w), :]
        v = v_ref[pl.ds(start, w), :]
        for pr in range(GROUP_B):
            cols = slice(pr * LANE, (pr + 1) * LANE)
            lo, hi = pr, pr + GROUP_B
            o, _, _ = _pair_attention(q_ref[rows, cols], k, v, bias_ref[var, lo],
                                      bias_ref[var, hi], sink_ref[lo], sink_ref[hi])
            o_ref[rows, cols] = o.astype(BF16)


def _attn_b(q, k, v, bias, sink, batch):
    S = q.shape[0] // batch
    nb = S // QBLOCK
    group = min(nb, ATTN_GROUP)
    seq = lambda t: t.reshape(batch, S, t.shape[-1])
    o = pl.pallas_call(
        functools.partial(_attn_b_kernel, radius=SWA_RADIUS, length=S, nb=nb, group=group),
        grid=(batch, nb // group),
        in_specs=[
            pl.BlockSpec(memory_space=pltpu.SMEM),
            pl.BlockSpec((None, group * QBLOCK, MIX_A), lambda b, i: (b, i, 0)),
            pl.BlockSpec((None, S, KV_B), lambda b, i: (b, 0, 0)),
            pl.BlockSpec((None, S, KV_B), lambda b, i: (b, 0, 0)),
            _resident(bias.shape, lambda b, i: (0, 0, 0, 0)),
        ],
        out_specs=pl.BlockSpec((None, group * QBLOCK, MIX_A), lambda b, i: (b, i, 0)),
        out_shape=jax.ShapeDtypeStruct((batch, S, MIX_A), BF16),
        compiler_params=_cparams(("parallel", "arbitrary")),
        name="attn_b",
    )(sink, seq(q), seq(k), seq(v), bias)
    return o.reshape(batch * S, MIX_A)


def _wo_kernel(x_ref, *refs):
    n_d = len(DILATIONS)
    o_refs, l_refs = refs[:n_d], refs[n_d:2 * n_d]
    ob_ref, w_ref, out_ref, oscr_ref, lscr_ref = refs[2 * n_d:]
    tm = x_ref.shape[0]
    n_blk = MIX_A // LANE
    for di, d in enumerate(DILATIONS):
        for r in range(d):
            rows = slice(None) if d == 1 else pl.ds(r, tm // d, stride=d)
            lscr_ref[di, rows, :] = l_refs[di][:, r * LANE:(r + 1) * LANE]
            for c in range(n_blk):
                src = o_refs[di][:, r * MIX_A + c * LANE:r * MIX_A + (c + 1) * LANE]
                oscr_ref[di * n_blk + c, rows, :] = src.astype(F32)
    lse = [lscr_ref[di] for di in range(n_d)]
    mx = functools.reduce(jnp.maximum, lse)
    ex = [jnp.exp(l - mx) for l in lse]
    inv = 1.0 / functools.reduce(jnp.add, ex)
    wts = [e * inv for e in ex]
    lane = lax.broadcasted_iota(jnp.int32, (tm, LANE), 1)
    acc = x_ref[...] + jnp.dot(ob_ref[...], w_ref[MIX_A:, :], preferred_element_type=F32)
    for c in range(n_blk):
        lo, hi = 2 * c * LSE_LANES, (2 * c + 1) * LSE_LANES
        merged = jnp.zeros((tm, LANE), F32)
        for di, wt in enumerate(wts):
            wfull = jnp.where(lane < HEAD_DIM, wt[:, lo:lo + 1], wt[:, hi:hi + 1])
            merged = merged + wfull * oscr_ref[di * n_blk + c]
        acc = acc + jnp.dot(merged.astype(BF16), w_ref[c * LANE:(c + 1) * LANE, :],
                            preferred_element_type=F32)
    out_ref[...] = acc


def _wo(x, outs, lses, ob, w, layer, *, tm=ROW_TILE):
    T, D = x.shape
    n_d = len(DILATIONS)
    row = lambda width: pl.BlockSpec((tm, width), lambda i: (i, 0))
    return pl.pallas_call(
        _wo_kernel,
        grid=(T // tm,),
        in_specs=([row(D)]
                  + [pl.BlockSpec((tm // d, d * MIX_A), lambda i: (i, 0)) for d in DILATIONS]
                  + [pl.BlockSpec((tm // d, d * LANE), lambda i: (i, 0)) for d in DILATIONS]
                  + [row(MIX_A), _layer_block(w, layer)]),
        out_specs=row(D),
        out_shape=jax.ShapeDtypeStruct((T, D), F32),
        scratch_shapes=[pltpu.VMEM((n_d * MIX_A // LANE, tm, LANE), F32),
                        pltpu.VMEM((n_d, tm, LANE), F32)],
        compiler_params=_cparams(("parallel",)),
        name="merge_wo",
    )(x, *outs, *lses, ob, w)


def _ple_kernel(x_ref, p_ref, g_ref, wg_ref, wp_ref, o_ref):
    x = x_ref[...]
    h = (x * _rms_scale(x) * g_ref[...]).astype(BF16)
    gate = jax.nn.sigmoid(jnp.dot(h, wg_ref[...], preferred_element_type=F32))
    proj = jnp.dot(p_ref[...].astype(BF16), wp_ref[...], preferred_element_type=F32)
    o_ref[...] = x + gate * proj


def _ple(x, p, g, w_gate, w_proj, layer, *, tm=ROW_TILE):
    T, D = x.shape
    return pl.pallas_call(
        _ple_kernel,
        grid=(T // tm,),
        in_specs=[
            pl.BlockSpec((tm, D), lambda i: (i, 0)),
            pl.BlockSpec((None, tm, p.shape[-1]), lambda i: (layer, i, 0)),
            _resident((None, 1, D), lambda i: (layer, 0, 0)),
            _layer_block(w_gate, layer),
            _layer_block(w_proj, layer),
        ],
        out_specs=pl.BlockSpec((tm, D), lambda i: (i, 0)),
        out_shape=jax.ShapeDtypeStruct((T, D), F32),
        compiler_params=_cparams(("parallel",)),
        name="ple",
    )(x, p, g, w_gate, w_proj)


def kernel(x, p, rel_bias, norm_ffn1, ffn1_w_in, ffn1_w_out, norm_mix, w_qkv, q_norm_a, k_norm_a, q_norm_b, k_norm_b, sink_b, w_o, norm_ffn2, ffn2_w_in, ffn2_w_out, norm_ple, w_ple_gate, w_ple_proj):
    B, S, D = x.shape
    depth = p.shape[0]
    T = B * S
    x = x.reshape(T, D)
    p = p.reshape(depth, T, p.shape[-1])

    bias_a = [_bias_tiles(rel_bias, window // (2 * d), d, 0) for window, d in DILATED_CONFIGS]
    bias_b = _bias_tiles(rel_bias, SWA_RADIUS, 1, N_HEADS)

    head_order = np.arange(N_HEADS).reshape(N_KV_B, GROUP_B).T.reshape(-1)
    b_cols = (head_order[:, None] * HEAD_DIM + np.arange(HEAD_DIM)[None, :]).reshape(-1)
    qkv_cols = np.arange(w_qkv.shape[-1])
    qkv_cols[3 * MIX_A:4 * MIX_A] = 3 * MIX_A + b_cols
    wo_rows = np.concatenate([np.arange(MIX_A), MIX_A + b_cols])

    scale = HEAD_DIM ** -0.5
    tile = lambda g, n: jnp.tile(g, (1, n))
    gains = jnp.concatenate([
        tile(q_norm_a * scale, N_HEADS), tile(k_norm_a, N_HEADS),
        jnp.ones((depth, MIX_A), F32),
        tile(q_norm_b * scale, N_HEADS), tile(k_norm_b, N_KV_B),
        jnp.ones((depth, KV_B), F32)], axis=1)[:, None]
    row = lambda g: g[:, None]
    bf = lambda w: w.astype(BF16)
    w_in1, w_out1, w_in2, w_out2 = bf(ffn1_w_in), bf(ffn1_w_out), bf(ffn2_w_in), bf(ffn2_w_out)
    w_qkv_b, w_o_b = bf(w_qkv[:, :, qkv_cols]), bf(w_o[:, wo_rows])
    w_gate_b, w_proj_b = bf(w_ple_gate), bf(w_ple_proj)

    for i in range(depth):
        x = _ffn(x, row(norm_ffn1), w_in1, w_out1, i)
        (qa, ka, va), (qb, kb, vb) = _qkv(x, row(norm_mix), w_qkv_b, gains, i)
        outs, lses = [], []
        for di, (window, d) in enumerate(DILATED_CONFIGS):
            o, lse = _attn_a(qa[di], ka[di], va[di], bias_a[di], B, window=window, dilation=d)
            outs.append(o)
            lses.append(lse)
        ob = _attn_b(qb, kb, vb, bias_b, sink_b[i], B)
        x = _wo(x, outs, lses, ob, w_o_b, i)
        x = _ffn(x, row(norm_ffn2), w_in2, w_out2, i)
        x = _ple(x, p, row(norm_ple), w_gate_b, w_proj_b, i)
    return x.reshape(B, S, D)
```

```python
import functools
import math

import jax
import jax.numpy as jnp
import numpy as np
from jax import lax
from jax.experimental import pallas as pl
from jax.experimental.pallas import tpu as pltpu

HEAD_DIM = 64
N_HEADS = 8
N_KV_B = 2
GROUP_B = N_HEADS // N_KV_B
MIX_A = N_HEADS * HEAD_DIM
KV_B = N_KV_B * HEAD_DIM
DILATED_CONFIGS = ((128, 1), (512, 4), (2048, 16))
DILATIONS = tuple(d for _, d in DILATED_CONFIGS)
SWA_RADIUS = 128
N_BUCKETS = 32
MAX_DISTANCE = 1024
EPS = 1e-6
NEG = -1e30
LOG2E = math.log2(math.e)
QBLOCK = 128
ATTN_GROUP = 8
LSE_LANES = 16
LANE = 128
FF_CHUNK = 256
ROW_TILE = 512

VMEM_LIMIT = 56 * 1024 * 1024

BF16 = jnp.bfloat16
F32 = jnp.float32


def _cparams(sem):
    return pltpu.CompilerParams(dimension_semantics=sem, vmem_limit_bytes=VMEM_LIMIT)


def _resident(shape, index_map):
    return pl.BlockSpec(shape, index_map, pipeline_mode=pl.Buffered(1))


def _layer_block(w, layer):
    return _resident((None,) + w.shape[1:], lambda i: (layer, 0, 0))


def _rms_scale(x):
    return lax.rsqrt(jnp.mean(x * x, axis=-1, keepdims=True) + EPS)


def _ffn_kernel(x_ref, g_ref, win_ref, wout_ref, o_ref, act_ref, *, d_ff):
    x = x_ref[...]
    h = (x * _rms_scale(x) * g_ref[...]).astype(BF16)
    for c in range(0, d_ff, FF_CHUNK):
        gate = jnp.dot(h, win_ref[:, c:c + FF_CHUNK], preferred_element_type=F32)
        up = jnp.dot(h, win_ref[:, d_ff + c:d_ff + c + FF_CHUNK], preferred_element_type=F32)
        act_ref[:, c:c + FF_CHUNK] = (gate * jax.nn.sigmoid(gate) * up).astype(BF16)
    o_ref[...] = x_ref[...] + 0.5 * jnp.dot(act_ref[...], wout_ref[...],
                                             preferred_element_type=F32)


def _ffn(x, g, w_in, w_out, layer, *, tm=ROW_TILE):
    T, D = x.shape
    d_ff = w_out.shape[1]
    return pl.pallas_call(
        functools.partial(_ffn_kernel, d_ff=d_ff),
        grid=(T // tm,),
        in_specs=[
            pl.BlockSpec((tm, D), lambda i: (i, 0)),
            _resident((None, 1, D), lambda i: (layer, 0, 0)),
            _layer_block(w_in, layer),
            _layer_block(w_out, layer),
        ],
        out_specs=pl.BlockSpec((tm, D), lambda i: (i, 0)),
        out_shape=jax.ShapeDtypeStruct((T, D), F32),
        scratch_shapes=[pltpu.VMEM((tm, d_ff), BF16)],
        compiler_params=_cparams(("parallel",)),
        name="ffn",
    )(x, g, w_in, w_out)


def _head_mean_sq(y):
    sq = y * y
    hi = sq.astype(BF16)
    lo = (sq - hi.astype(F32)).astype(BF16)
    r = (lax.broadcasted_iota(jnp.int32, (2 * LANE, LANE), 0) % LANE) // HEAD_DIM
    c = lax.broadcasted_iota(jnp.int32, (2 * LANE, LANE), 1) // HEAD_DIM
    seg = jnp.where(r == c, 1.0 / HEAD_DIM, 0.0).astype(BF16)
    return jnp.dot(jnp.concatenate([hi, lo], axis=1), seg, preferred_element_type=F32)


def _qk_normed(y, gain):
    return y * lax.rsqrt(_head_mean_sq(y) + EPS) * gain


def _qkv_kernel(x_ref, g_ref, w_ref, gq_ref, *refs):
    n_d = len(DILATIONS)
    a_refs, (qb_ref, kb_ref, vb_ref, scr_ref) = refs[:3 * n_d], refs[3 * n_d:]
    x = x_ref[...]
    tm = x.shape[0]
    h = (x * _rms_scale(x) * g_ref[...]).astype(BF16)
    n_blk = MIX_A // LANE
    col = 0
    for gi, normed in enumerate((True, True, False)):
        y = jnp.dot(h, w_ref[:, col:col + MIX_A], preferred_element_type=F32)
        for c in range(n_blk):
            yc = y[:, c * LANE:(c + 1) * LANE]
            if normed:
                yc = _qk_normed(yc, gq_ref[:, col + c * LANE:col + (c + 1) * LANE])
            scr_ref[gi * n_blk + c] = yc
        for di, d in enumerate(DILATIONS):
            out = a_refs[gi * n_d + di]
            for r in range(d):
                for c in range(n_blk):
                    slab = scr_ref.at[gi * n_blk + c]
                    rows = slab[...] if d == 1 else slab[pl.ds(r, tm // d, stride=d), :]
                    out[:, r * MIX_A + c * LANE:r * MIX_A + (c + 1) * LANE] = rows.astype(BF16)
        col += MIX_A
    yq = jnp.dot(h, w_ref[:, col:col + MIX_A], preferred_element_type=F32)
    for c in range(n_blk):
        cols = slice(c * LANE, (c + 1) * LANE)
        qb_ref[:, cols] = _qk_normed(yq[:, cols], gq_ref[:, col + c * LANE:col + (c + 1) * LANE]
                                     ).astype(BF16)
    col += MIX_A
    ykv = jnp.dot(h, w_ref[:, col:col + 2 * KV_B], preferred_element_type=F32)
    lane = lax.broadcasted_iota(jnp.int32, (tm, LANE), 1)
    for ref, t in ((kb_ref, _qk_normed(ykv[:, :KV_B], gq_ref[:, col:col + KV_B])),
                   (vb_ref, ykv[:, KV_B:])):
        swapped = pltpu.roll(t, HEAD_DIM, 1)
        ref[:, :LANE] = jnp.where(lane < HEAD_DIM, t, swapped).astype(BF16)
        ref[:, LANE:] = jnp.where(lane < HEAD_DIM, swapped, t).astype(BF16)


def _qkv(x, g, w, gains, layer, *, tm=ROW_TILE):
    T, D = x.shape
    a_shapes = [(d, MIX_A) for _ in range(3) for d in DILATIONS]
    b_widths = [MIX_A, N_KV_B * LANE, N_KV_B * LANE]
    shapes = [(T // d, d * c) for d, c in a_shapes] + [(T, c) for c in b_widths]
    blocks = [(tm // d, d * c) for d, c in a_shapes] + [(tm, c) for c in b_widths]
    outs = pl.pallas_call(
        _qkv_kernel,
        grid=(T // tm,),
        in_specs=[
            pl.BlockSpec((tm, D), lambda i: (i, 0)),
            _resident((None, 1, D), lambda i: (layer, 0, 0)),
            _layer_block(w, layer),
            _resident((None, 1, gains.shape[-1]), lambda i: (layer, 0, 0)),
        ],
        out_specs=[pl.BlockSpec(b, lambda i: (i, 0)) for b in blocks],
        out_shape=[jax.ShapeDtypeStruct(s, BF16) for s in shapes],
        scratch_shapes=[pltpu.VMEM((3 * MIX_A // LANE, tm, LANE), F32)],
        compiler_params=_cparams(("parallel",)),
        name="qkv",
    )(x, g, w, gains)
    n_d = len(DILATIONS)
    return (outs[:n_d], outs[n_d:2 * n_d], outs[2 * n_d:3 * n_d]), outs[3 * n_d:]


def _t5_bucket_np(rel):
    half = N_BUCKETS // 2
    max_exact = half // 2
    ret = np.where(rel > 0, half, 0)
    n = np.abs(rel)
    nf = np.maximum(n, 1).astype(np.float64)
    large = max_exact + (np.log(nf / max_exact) / math.log(MAX_DISTANCE / max_exact)
                         * (half - max_exact)).astype(np.int64)
    large = np.minimum(large, half - 1)
    return ret + np.where(n < max_exact, n, large)


def _bucket_tiles(radius, dilation):
    w = QBLOCK + 2 * radius
    key = np.arange(w)[:, None]
    q = np.arange(QBLOCK)[None, :]
    tiles = []
    for off in (0, -radius, -2 * radius):
        rel = off + key - q
        tiles.append(np.where(np.abs(rel) <= radius, _t5_bucket_np(rel * dilation), -1))
    return np.stack(tiles).astype(np.int32)


def _bias_kernel(rb_ref, idx_ref, o_ref, *, head0, buckets):
    idx = idx_ref[...]
    for h in range(N_HEADS):
        tile = jnp.full(idx.shape, NEG, F32)
        for b in buckets:
            tile = jnp.where(idx == b, rb_ref[b, head0 + h] * LOG2E, tile)
        o_ref[h] = tile


def _bias_tiles(rel_bias, radius, dilation, head0):
    idx_np = _bucket_tiles(radius, dilation)
    buckets = tuple(int(b) for b in np.unique(idx_np[idx_np >= 0]))
    nv, w, bq = idx_np.shape
    return pl.pallas_call(
        functools.partial(_bias_kernel, head0=head0, buckets=buckets),
        grid=(nv,),
        in_specs=[
            pl.BlockSpec(memory_space=pltpu.SMEM),
            pl.BlockSpec((None, w, bq), lambda v: (v, 0, 0)),
        ],
        out_specs=pl.BlockSpec((None, N_HEADS, w, bq), lambda v: (v, 0, 0, 0)),
        out_shape=jax.ShapeDtypeStruct((nv, N_HEADS, w, bq), F32),
        compiler_params=_cparams(("arbitrary",)),
        name="bias_tiles",
    )(rel_bias, jnp.asarray(idx_np))


def _window(i, nb, radius, length):
    w = QBLOCK + 2 * radius
    start = pl.multiple_of(jnp.clip(i * QBLOCK - radius, 0, length - w), radius)
    var = jnp.where(i == 0, 0, jnp.where(i == nb - 1, 2, 1))
    return start, var


_NT = (((1,), (1,)), ((), ()))
_TN = (((0,), (0,)), ((), ()))


def _head_softmax(kp, qh, vp, bias, sink=None):
    s = lax.dot_general(kp, qh, _NT, preferred_element_type=F32) + bias
    m = jnp.max(s, axis=0, keepdims=True)
    if sink is not None:
        m = jnp.maximum(m, sink)
    p = jnp.exp2(s - m)
    den = jnp.sum(p, axis=0, keepdims=True)
    if sink is not None:
        den = den + jnp.exp2(sink - m)
    o = lax.dot_general(vp, p.astype(BF16), _TN, preferred_element_type=F32)
    return o * (1.0 / den), m + jnp.log2(den)


def _pair_attention(qp, kp, vp, bias_lo, bias_hi, sink_lo=None, sink_hi=None):
    lane = lax.broadcasted_iota(jnp.int32, qp.shape, 1)
    zero = jnp.zeros_like(qp)
    o_lo, lse_lo = _head_softmax(kp, jnp.where(lane < HEAD_DIM, qp, zero), vp, bias_lo, sink_lo)
    o_hi, lse_hi = _head_softmax(kp, jnp.where(lane >= HEAD_DIM, qp, zero), vp, bias_hi, sink_hi)
    o_t = jnp.concatenate([o_lo[:HEAD_DIM], o_hi[HEAD_DIM:]], axis=0)
    return o_t.T, lse_lo, lse_hi


def _attn_a_kernel(q_ref, k_ref, v_ref, bias_ref, o_ref, lse_ref, *,
                   radius, length, nb, group, classes):
    w = QBLOCK + 2 * radius
    for cl, g in [(cl, g) for cl in range(classes) for g in range(group)]:
        start, var = _window(pl.program_id(2) * group + g, nb, radius, length)
        rows = slice(g * QBLOCK, (g + 1) * QBLOCK)
        lses = []
        for pr in range(N_HEADS // 2):
            cols = slice(cl * MIX_A + pr * LANE, cl * MIX_A + (pr + 1) * LANE)
            o, lse_lo, lse_hi = _pair_attention(
                q_ref[rows, cols], k_ref[pl.ds(start, w), cols], v_ref[pl.ds(start, w), cols],
                bias_ref[var, 2 * pr], bias_ref[var, 2 * pr + 1])
            o_ref[rows, cols] = o.astype(BF16)
            lses += [lse_lo, lse_hi]
        lse_t = jnp.concatenate([jnp.broadcast_to(l, (LSE_LANES, QBLOCK)) for l in lses], axis=0)
        lse_ref[rows, cl * LANE:(cl + 1) * LANE] = lse_t.T


def _attn_a(q, k, v, bias, batch, *, window, dilation):
    d = dilation
    L = q.shape[0] // batch
    nb = L // QBLOCK
    radius = window // (2 * d)
    group = min(nb, ATTN_GROUP)
    classes = min(d, ATTN_GROUP // group)
    seq = lambda t: t.reshape(batch, L, t.shape[-1])
    o, lse = pl.pallas_call(
        functools.partial(_attn_a_kernel, radius=radius, length=L, nb=nb, group=group,
                          classes=classes),
        grid=(batch, d // classes, nb // group),
        in_specs=[
            pl.BlockSpec((None, group * QBLOCK, classes * MIX_A), lambda b, r, i: (b, i, r)),
            pl.BlockSpec((None, L, classes * MIX_A), lambda b, r, i: (b, 0, r)),
            pl.BlockSpec((None, L, classes * MIX_A), lambda b, r, i: (b, 0, r)),
            _resident(bias.shape, lambda b, r, i: (0, 0, 0, 0)),
        ],
        out_specs=[
            pl.BlockSpec((None, group * QBLOCK, classes * MIX_A), lambda b, r, i: (b, i, r)),
            pl.BlockSpec((None, group * QBLOCK, classes * LANE), lambda b, r, i: (b, i, r)),
        ],
        out_shape=[
            jax.ShapeDtypeStruct((batch, L, d * MIX_A), BF16),
            jax.ShapeDtypeStruct((batch, L, d * LANE), F32),
        ],
        compiler_params=_cparams(("parallel", "parallel", "arbitrary")),
        name=f"attn_a_d{d}",
    )(seq(q), seq(k), seq(v), bias)
    return o.reshape(batch * L, d * MIX_A), lse.reshape(batch * L, d * LANE)


def _attn_b_kernel(sink_ref, q_ref, k_ref, v_ref, bias_ref, o_ref, *, radius, length, nb, group):
    w = QBLOCK + 2 * radius
    for g in range(group):
        start, var = _window(pl.program_id(1) * group + g, nb, radius, length)
        rows = slice(g * QBLOCK, (g + 1) * QBLOCK)
        for pr in range(N_HEADS // 2):
            cols = slice(pr * LANE, (pr + 1) * LANE)
            kv_cols = slice(2 * pr // GROUP_B * LANE, (2 * pr // GROUP_B + 1) * LANE)
            lo, hi = 2 * pr, 2 * pr + 1
            o, _, _ = _pair_attention(
                q_ref[rows, cols], k_ref[pl.ds(start, w), kv_cols], v_ref[pl.ds(start, w), kv_cols],
                bias_ref[var, lo], bias_ref[var, hi], sink_ref[lo] * LOG2E, sink_ref[hi] * LOG2E)
            o_ref[rows, cols] = o.astype(BF16)


def _attn_b(q, k, v, bias, sink, batch):
    S = q.shape[0] // batch
    nb = S // QBLOCK
    group = min(nb, ATTN_GROUP)
    seq = lambda t: t.reshape(batch, S, t.shape[-1])
    o = pl.pallas_call(
        functools.partial(_attn_b_kernel, radius=SWA_RADIUS, length=S, nb=nb, group=group),
        grid=(batch, nb // group),
        in_specs=[
            pl.BlockSpec(memory_space=pltpu.SMEM),
            pl.BlockSpec((None, group * QBLOCK, MIX_A), lambda b, i: (b, i, 0)),
            pl.BlockSpec((None, S, N_KV_B * LANE), lambda b, i: (b, 0, 0)),
            pl.BlockSpec((None, S, N_KV_B * LANE), lambda b, i: (b, 0, 0)),
            _resident(bias.shape, lambda b, i: (0, 0, 0, 0)),
        ],
        out_specs=pl.BlockSpec((None, group * QBLOCK, MIX_A), lambda b, i: (b, i, 0)),
        out_shape=jax.ShapeDtypeStruct((batch, S, MIX_A), BF16),
        compiler_params=_cparams(("parallel", "arbitrary")),
        name="attn_b",
    )(sink, seq(q), seq(k), seq(v), bias)
    return o.reshape(batch * S, MIX_A)


def _wo_kernel(x_ref, *refs):
    n_d = len(DILATIONS)
    o_refs, l_refs = refs[:n_d], refs[n_d:2 * n_d]
    ob_ref, w_ref, out_ref, oscr_ref, lscr_ref = refs[2 * n_d:]
    tm = x_ref.shape[0]
    n_blk = MIX_A // LANE
    for di, d in enumerate(DILATIONS):
        for r in range(d):
            rows = slice(None) if d == 1 else pl.ds(r, tm // d, stride=d)
            lscr_ref[di, rows, :] = l_refs[di][:, r * LANE:(r + 1) * LANE]
            for c in range(n_blk):
                src = o_refs[di][:, r * MIX_A + c * LANE:r * MIX_A + (c + 1) * LANE]
                oscr_ref[di * n_blk + c, rows, :] = src.astype(F32)
    lse = [lscr_ref[di] for di in range(n_d)]
    mx = functools.reduce(jnp.maximum, lse)
    ex = [jnp.exp2(l - mx) for l in lse]
    inv = 1.0 / functools.reduce(jnp.add, ex)
    wts = [e * inv for e in ex]
    lane = lax.broadcasted_iota(jnp.int32, (tm, LANE), 1)
    acc = x_ref[...] + jnp.dot(ob_ref[...], w_ref[MIX_A:, :], preferred_element_type=F32)
    for c in range(n_blk):
        lo, hi = 2 * c * LSE_LANES, (2 * c + 1) * LSE_LANES
        merged = jnp.zeros((tm, LANE), F32)
        for di, wt in enumerate(wts):
            wfull = jnp.where(lane < HEAD_DIM, wt[:, lo:lo + 1], wt[:, hi:hi + 1])
            merged = merged + wfull * oscr_ref[di * n_blk + c]
        acc = acc + jnp.dot(merged.astype(BF16), w_ref[c * LANE:(c + 1) * LANE, :],
                            preferred_element_type=F32)
    out_ref[...] = acc


def _wo(x, outs, lses, ob, w, layer, *, tm=ROW_TILE):
    T, D = x.shape
    n_d = len(DILATIONS)
    row = lambda width: pl.BlockSpec((tm, width), lambda i: (i, 0))
    return pl.pallas_call(
        _wo_kernel,
        grid=(T // tm,),
        in_specs=([row(D)]
                  + [pl.BlockSpec((tm // d, d * MIX_A), lambda i: (i, 0)) for d in DILATIONS]
                  + [pl.BlockSpec((tm // d, d * LANE), lambda i: (i, 0)) for d in DILATIONS]
                  + [row(MIX_A), _layer_block(w, layer)]),
        out_specs=row(D),
        out_shape=jax.ShapeDtypeStruct((T, D), F32),
        scratch_shapes=[pltpu.VMEM((n_d * MIX_A // LANE, tm, LANE), F32),
                        pltpu.VMEM((n_d, tm, LANE), F32)],
        compiler_params=_cparams(("parallel",)),
        name="merge_wo",
    )(x, *outs, *lses, ob, w)


def _ple_kernel(x_ref, p_ref, g_ref, wg_ref, wp_ref, o_ref):
    x = x_ref[...]
    h = (x * _rms_scale(x) * g_ref[...]).astype(BF16)
    gate = jax.nn.sigmoid(jnp.dot(h, wg_ref[...], preferred_element_type=F32))
    proj = jnp.dot(p_ref[...].astype(BF16), wp_ref[...], preferred_element_type=F32)
    o_ref[...] = x + gate * proj


def _ple(x, p, g, w_gate, w_proj, layer, *, tm=ROW_TILE):
    T, D = x.shape
    return pl.pallas_call(
        _ple_kernel,
        grid=(T // tm,),
        in_specs=[
            pl.BlockSpec((tm, D), lambda i: (i, 0)),
            pl.BlockSpec((None, tm, p.shape[-1]), lambda i: (layer, i, 0)),
            _resident((None, 1, D), lambda i: (layer, 0, 0)),
            _layer_block(w_gate, layer),
            _layer_block(w_proj, layer),
        ],
        out_specs=pl.BlockSpec((tm, D), lambda i: (i, 0)),
        out_shape=jax.ShapeDtypeStruct((T, D), F32),
        compiler_params=_cparams(("parallel",)),
        name="ple",
    )(x, p, g, w_gate, w_proj)


def kernel(x, p, rel_bias, norm_ffn1, ffn1_w_in, ffn1_w_out, norm_mix, w_qkv, q_norm_a, k_norm_a, q_norm_b, k_norm_b, sink_b, w_o, norm_ffn2, ffn2_w_in, ffn2_w_out, norm_ple, w_ple_gate, w_ple_proj):
    B, S, D = x.shape
    depth = p.shape[0]
    T = B * S
    x = x.reshape(T, D)
    p = p.reshape(depth, T, p.shape[-1])

    bias_a = [_bias_tiles(rel_bias, window // (2 * d), d, 0) for window, d in DILATED_CONFIGS]
    bias_b = _bias_tiles(rel_bias, SWA_RADIUS, 1, N_HEADS)

    scale = HEAD_DIM ** -0.5 * LOG2E
    tile = lambda g, n: jnp.tile(g, (1, n))
    gains = jnp.concatenate([
        tile(q_norm_a * scale, N_HEADS), tile(k_norm_a, N_HEADS),
        jnp.ones((depth, MIX_A), F32),
        tile(q_norm_b * scale, N_HEADS), tile(k_norm_b, N_KV_B),
        jnp.ones((depth, KV_B), F32)], axis=1)[:, None]
    row = lambda g: g[:, None]
    bf = lambda w: w.astype(BF16)
    w_in1, w_out1, w_in2, w_out2 = bf(ffn1_w_in), bf(ffn1_w_out), bf(ffn2_w_in), bf(ffn2_w_out)
    w_qkv_b, w_o_b = bf(w_qkv), bf(w_o)
    w_gate_b, w_proj_b = bf(w_ple_gate), bf(w_ple_proj)

    for i in range(depth):
        x = _ffn(x, row(norm_ffn1), w_in1, w_out1, i)
        (qa, ka, va), (qb, kb, vb) = _qkv(x, row(norm_mix), w_qkv_b, gains, i)
        outs, lses = [], []
        for di, (window, d) in enumerate(DILATED_CONFIGS):
            o, lse = _attn_a(qa[di], ka[di], va[di], bias_a[di], B, window=window, dilation=d)
            outs.append(o)
            lses.append(lse)
        ob = _attn_b(qb, kb, vb, bias_b, sink_b[i], B)
        x = _wo(x, outs, lses, ob, w_o_b, i)
        x = _ffn(x, row(norm_ffn2), w_in2, w_out2, i)
        x = _ple(x, p, row(norm_ple), w_gate_b, w_proj_b, i)
    return x.reshape(B, S, D)
```

```python
import functools
import math

import jax
import jax.numpy as jnp
import numpy as np
from jax import lax
from jax.experimental import pallas as pl
from jax.experimental.pallas import tpu as pltpu

HEAD_DIM = 64
N_HEADS = 8
N_KV_B = 2
GROUP_B = N_HEADS // N_KV_B
MIX_A = N_HEADS * HEAD_DIM
KV_B = N_KV_B * HEAD_DIM
DILATED_CONFIGS = ((128, 1), (512, 4), (2048, 16))
DILATIONS = tuple(d for _, d in DILATED_CONFIGS)
SWA_RADIUS = 128
N_BUCKETS = 32
MAX_DISTANCE = 1024
EPS = 1e-6
NEG = -1e30
LOG2E = math.log2(math.e)
QBLOCK = 128
ATTN_GROUP = 8
LSE_LANES = 16
LANE = 128
FF_CHUNK = 256
ROW_TILE = 512

VMEM_LIMIT = 56 * 1024 * 1024

BF16 = jnp.bfloat16
F32 = jnp.float32


def _cparams(sem):
    return pltpu.CompilerParams(dimension_semantics=sem, vmem_limit_bytes=VMEM_LIMIT)


def _resident(shape, index_map):
    return pl.BlockSpec(shape, index_map, pipeline_mode=pl.Buffered(1))


def _layer_block(w, layer):
    return _resident((None,) + w.shape[1:], lambda i: (layer, 0, 0))


def _rms_scale(x):
    return lax.rsqrt(jnp.mean(x * x, axis=-1, keepdims=True) + EPS)


def _swiglu(h, win_ref, wout_ref, act_ref, fillers=()):
    d_ff = act_ref.shape[-1]
    chunks = range(0, d_ff, FF_CHUNK)
    fillers = list(fillers)
    per_chunk = -(-len(fillers) // len(chunks))
    for c in chunks:
        gate = jnp.dot(h, win_ref[:, c:c + FF_CHUNK], preferred_element_type=F32)
        up = jnp.dot(h, win_ref[:, d_ff + c:d_ff + c + FF_CHUNK], preferred_element_type=F32)
        act_ref[:, c:c + FF_CHUNK] = (gate * jax.nn.sigmoid(gate) * up).astype(BF16)
        for fill in fillers[:per_chunk]:
            fill()
        del fillers[:per_chunk]
    return jnp.dot(act_ref[...], wout_ref[...], preferred_element_type=F32)


def _ffn_kernel(x_ref, g_ref, win_ref, wout_ref, o_ref, act_ref):
    x = x_ref[...]
    h = (x * _rms_scale(x) * g_ref[...]).astype(BF16)
    o_ref[...] = x_ref[...] + 0.5 * _swiglu(h, win_ref, wout_ref, act_ref)


def _ffn(x, g, w_in, w_out, layer, *, tm=ROW_TILE):
    T, D = x.shape
    d_ff = w_out.shape[1]
    return pl.pallas_call(
        _ffn_kernel,
        grid=(T // tm,),
        in_specs=[
            pl.BlockSpec((tm, D), lambda i: (i, 0)),
            _resident((None, 1, D), lambda i: (layer, 0, 0)),
            _layer_block(w_in, layer),
            _layer_block(w_out, layer),
        ],
        out_specs=pl.BlockSpec((tm, D), lambda i: (i, 0)),
        out_shape=jax.ShapeDtypeStruct((T, D), F32),
        scratch_shapes=[pltpu.VMEM((tm, d_ff), BF16)],
        compiler_params=_cparams(("parallel",)),
        name="ffn",
    )(x, g, w_in, w_out)


def _head_mean_sq(y):
    sq = y * y
    hi = sq.astype(BF16)
    lo = (sq - hi.astype(F32)).astype(BF16)
    r = (lax.broadcasted_iota(jnp.int32, (2 * LANE, LANE), 0) % LANE) // HEAD_DIM
    c = lax.broadcasted_iota(jnp.int32, (2 * LANE, LANE), 1) // HEAD_DIM
    seg = jnp.where(r == c, 1.0 / HEAD_DIM, 0.0).astype(BF16)
    return jnp.dot(jnp.concatenate([hi, lo], axis=1), seg, preferred_element_type=F32)


def _qk_normed(y, gain):
    return y * lax.rsqrt(_head_mean_sq(y) + EPS) * gain


def _qkv_kernel(x_ref, g_ref, w_ref, gq_ref, *refs):
    n_d = len(DILATIONS)
    a_refs, (qb_ref, kb_ref, vb_ref, scr_ref) = refs[:3 * n_d], refs[3 * n_d:]
    x = x_ref[...]
    tm = x.shape[0]
    h = (x * _rms_scale(x) * g_ref[...]).astype(BF16)
    n_blk = MIX_A // LANE
    col = 0
    for gi, normed in enumerate((True, True, False)):
        y = jnp.dot(h, w_ref[:, col:col + MIX_A], preferred_element_type=F32)
        for c in range(n_blk):
            yc = y[:, c * LANE:(c + 1) * LANE]
            if normed:
                yc = _qk_normed(yc, gq_ref[:, col + c * LANE:col + (c + 1) * LANE])
            scr_ref[gi * n_blk + c] = yc
        for di, d in enumerate(DILATIONS):
            out = a_refs[gi * n_d + di]
            for r in range(d):
                for c in range(n_blk):
                    slab = scr_ref.at[gi * n_blk + c]
                    rows = slab[...] if d == 1 else slab[pl.ds(r, tm // d, stride=d), :]
                    out[:, r * MIX_A + c * LANE:r * MIX_A + (c + 1) * LANE] = rows.astype(BF16)
        col += MIX_A
    yq = jnp.dot(h, w_ref[:, col:col + MIX_A], preferred_element_type=F32)
    for c in range(n_blk):
        cols = slice(c * LANE, (c + 1) * LANE)
        qb_ref[:, cols] = _qk_normed(yq[:, cols], gq_ref[:, col + c * LANE:col + (c + 1) * LANE]
                                     ).astype(BF16)
    col += MIX_A
    ykv = jnp.dot(h, w_ref[:, col:col + 2 * KV_B], preferred_element_type=F32)
    lane = lax.broadcasted_iota(jnp.int32, (tm, LANE), 1)
    for ref, t in ((kb_ref, _qk_normed(ykv[:, :KV_B], gq_ref[:, col:col + KV_B])),
                   (vb_ref, ykv[:, KV_B:])):
        swapped = pltpu.roll(t, HEAD_DIM, 1)
        ref[:, :LANE] = jnp.where(lane < HEAD_DIM, t, swapped).astype(BF16)
        ref[:, LANE:] = jnp.where(lane < HEAD_DIM, swapped, t).astype(BF16)


def _qkv(x, g, w, gains, layer, *, tm=ROW_TILE):
    T, D = x.shape
    a_shapes = [(d, MIX_A) for _ in range(3) for d in DILATIONS]
    b_widths = [MIX_A, N_KV_B * LANE, N_KV_B * LANE]
    shapes = [(T // d, d * c) for d, c in a_shapes] + [(T, c) for c in b_widths]
    blocks = [(tm // d, d * c) for d, c in a_shapes] + [(tm, c) for c in b_widths]
    outs = pl.pallas_call(
        _qkv_kernel,
        grid=(T // tm,),
        in_specs=[
            pl.BlockSpec((tm, D), lambda i: (i, 0)),
            _resident((None, 1, D), lambda i: (layer, 0, 0)),
            _layer_block(w, layer),
            _resident((None, 1, gains.shape[-1]), lambda i: (layer, 0, 0)),
        ],
        out_specs=[pl.BlockSpec(b, lambda i: (i, 0)) for b in blocks],
        out_shape=[jax.ShapeDtypeStruct(s, BF16) for s in shapes],
        scratch_shapes=[pltpu.VMEM((3 * MIX_A // LANE, tm, LANE), F32)],
        compiler_params=_cparams(("parallel",)),
        name="qkv",
    )(x, g, w, gains)
    n_d = len(DILATIONS)
    return (outs[:n_d], outs[n_d:2 * n_d], outs[2 * n_d:3 * n_d]), outs[3 * n_d:]


def _t5_bucket_np(rel):
    half = N_BUCKETS // 2
    max_exact = half // 2
    ret = np.where(rel > 0, half, 0)
    n = np.abs(rel)
    nf = np.maximum(n, 1).astype(np.float64)
    large = max_exact + (np.log(nf / max_exact) / math.log(MAX_DISTANCE / max_exact)
                         * (half - max_exact)).astype(np.int64)
    large = np.minimum(large, half - 1)
    return ret + np.where(n < max_exact, n, large)


def _bucket_tiles(radius, dilation):
    w = QBLOCK + 2 * radius
    key = np.arange(w)[:, None]
    q = np.arange(QBLOCK)[None, :]
    tiles = []
    for off in (0, -radius, -2 * radius):
        rel = off + key - q
        tiles.append(np.where(np.abs(rel) <= radius, _t5_bucket_np(rel * dilation), -1))
    return np.stack(tiles).astype(np.int32)


def _bias_kernel(rb_ref, idx_ref, o_ref, *, head0, buckets):
    idx = idx_ref[...]
    for h in range(N_HEADS):
        tile = jnp.full(idx.shape, NEG, F32)
        for b in buckets:
            tile = jnp.where(idx == b, rb_ref[b, head0 + h] * LOG2E, tile)
        o_ref[h] = tile


def _bias_tiles(rel_bias, radius, dilation, head0):
    idx_np = _bucket_tiles(radius, dilation)
    buckets = tuple(int(b) for b in np.unique(idx_np[idx_np >= 0]))
    nv, w, bq = idx_np.shape
    return pl.pallas_call(
        functools.partial(_bias_kernel, head0=head0, buckets=buckets),
        grid=(nv,),
        in_specs=[
            pl.BlockSpec(memory_space=pltpu.SMEM),
            pl.BlockSpec((None, w, bq), lambda v: (v, 0, 0)),
        ],
        out_specs=pl.BlockSpec((None, N_HEADS, w, bq), lambda v: (v, 0, 0, 0)),
        out_shape=jax.ShapeDtypeStruct((nv, N_HEADS, w, bq), F32),
        compiler_params=_cparams(("arbitrary",)),
        name="bias_tiles",
    )(rel_bias, jnp.asarray(idx_np))


def _window(i, nb, radius, length):
    w = QBLOCK + 2 * radius
    start = pl.multiple_of(jnp.clip(i * QBLOCK - radius, 0, length - w), radius)
    var = jnp.where(i == 0, 0, jnp.where(i == nb - 1, 2, 1))
    return start, var


_NT = (((1,), (1,)), ((), ()))
_TN = (((0,), (0,)), ((), ()))


def _head_softmax(kp, qh, vp, bias, sink=None):
    s = lax.dot_general(kp, qh, _NT, preferred_element_type=F32) + bias
    m = jnp.max(s, axis=0, keepdims=True)
    if sink is not None:
        m = jnp.maximum(m, sink)
    p = jnp.exp2(s - m)
    den = jnp.sum(p, axis=0, keepdims=True)
    if sink is not None:
        den = den + jnp.exp2(sink - m)
    o = lax.dot_general(vp, p.astype(BF16), _TN, preferred_element_type=F32)
    return o * (1.0 / den), m + jnp.log2(den)


def _pair_attention(qp, kp, vp, bias_lo, bias_hi, sink_lo=None, sink_hi=None):
    lane = lax.broadcasted_iota(jnp.int32, qp.shape, 1)
    zero = jnp.zeros_like(qp)
    o_lo, lse_lo = _head_softmax(kp, jnp.where(lane < HEAD_DIM, qp, zero), vp, bias_lo, sink_lo)
    o_hi, lse_hi = _head_softmax(kp, jnp.where(lane >= HEAD_DIM, qp, zero), vp, bias_hi, sink_hi)
    o_t = jnp.concatenate([o_lo[:HEAD_DIM], o_hi[HEAD_DIM:]], axis=0)
    return o_t.T, lse_lo, lse_hi


def _attn_a_kernel(q_ref, k_ref, v_ref, bias_ref, o_ref, lse_ref, *,
                   radius, length, nb, group, classes):
    w = QBLOCK + 2 * radius
    for cl, g in [(cl, g) for cl in range(classes) for g in range(group)]:
        start, var = _window(pl.program_id(2) * group + g, nb, radius, length)
        rows = slice(g * QBLOCK, (g + 1) * QBLOCK)
        lses = []
        for pr in range(N_HEADS // 2):
            cols = slice(cl * MIX_A + pr * LANE, cl * MIX_A + (pr + 1) * LANE)
            o, lse_lo, lse_hi = _pair_attention(
                q_ref[rows, cols], k_ref[pl.ds(start, w), cols], v_ref[pl.ds(start, w), cols],
                bias_ref[var, 2 * pr], bias_ref[var, 2 * pr + 1])
            o_ref[rows, cols] = o.astype(BF16)
            lses += [lse_lo, lse_hi]
        lse_t = jnp.concatenate([jnp.broadcast_to(l, (LSE_LANES, QBLOCK)) for l in lses], axis=0)
        lse_ref[rows, cl * LANE:(cl + 1) * LANE] = lse_t.T


def _attn_a(q, k, v, bias, batch, *, window, dilation):
    d = dilation
    L = q.shape[0] // batch
    nb = L // QBLOCK
    radius = window // (2 * d)
    group = min(nb, ATTN_GROUP)
    classes = min(d, ATTN_GROUP // group)
    seq = lambda t: t.reshape(batch, L, t.shape[-1])
    o, lse = pl.pallas_call(
        functools.partial(_attn_a_kernel, radius=radius, length=L, nb=nb, group=group,
                          classes=classes),
        grid=(batch, d // classes, nb // group),
        in_specs=[
            pl.BlockSpec((None, group * QBLOCK, classes * MIX_A), lambda b, r, i: (b, i, r)),
            pl.BlockSpec((None, L, classes * MIX_A), lambda b, r, i: (b, 0, r)),
            pl.BlockSpec((None, L, classes * MIX_A), lambda b, r, i: (b, 0, r)),
            _resident(bias.shape, lambda b, r, i: (0, 0, 0, 0)),
        ],
        out_specs=[
            pl.BlockSpec((None, group * QBLOCK, classes * MIX_A), lambda b, r, i: (b, i, r)),
            pl.BlockSpec((None, group * QBLOCK, classes * LANE), lambda b, r, i: (b, i, r)),
        ],
        out_shape=[
            jax.ShapeDtypeStruct((batch, L, d * MIX_A), BF16),
            jax.ShapeDtypeStruct((batch, L, d * LANE), F32),
        ],
        compiler_params=_cparams(("parallel", "parallel", "arbitrary")),
        name=f"attn_a_d{d}",
    )(seq(q), seq(k), seq(v), bias)
    return o.reshape(batch * L, d * MIX_A), lse.reshape(batch * L, d * LANE)


def _attn_b_kernel(sink_ref, q_ref, k_ref, v_ref, bias_ref, o_ref, *, radius, length, nb, group):
    w = QBLOCK + 2 * radius
    for g in range(group):
        start, var = _window(pl.program_id(1) * group + g, nb, radius, length)
        rows = slice(g * QBLOCK, (g + 1) * QBLOCK)
        for pr in range(N_HEADS // 2):
            cols = slice(pr * LANE, (pr + 1) * LANE)
            kv_cols = slice(2 * pr // GROUP_B * LANE, (2 * pr // GROUP_B + 1) * LANE)
            lo, hi = 2 * pr, 2 * pr + 1
            o, _, _ = _pair_attention(
                q_ref[rows, cols], k_ref[pl.ds(start, w), kv_cols], v_ref[pl.ds(start, w), kv_cols],
                bias_ref[var, lo], bias_ref[var, hi], sink_ref[lo] * LOG2E, sink_ref[hi] * LOG2E)
            o_ref[rows, cols] = o.astype(BF16)


def _attn_b(q, k, v, bias, sink, batch):
    S = q.shape[0] // batch
    nb = S // QBLOCK
    group = min(nb, ATTN_GROUP)
    seq = lambda t: t.reshape(batch, S, t.shape[-1])
    o = pl.pallas_call(
        functools.partial(_attn_b_kernel, radius=SWA_RADIUS, length=S, nb=nb, group=group),
        grid=(batch, nb // group),
        in_specs=[
            pl.BlockSpec(memory_space=pltpu.SMEM),
            pl.BlockSpec((None, group * QBLOCK, MIX_A), lambda b, i: (b, i, 0)),
            pl.BlockSpec((None, S, N_KV_B * LANE), lambda b, i: (b, 0, 0)),
            pl.BlockSpec((None, S, N_KV_B * LANE), lambda b, i: (b, 0, 0)),
            _resident(bias.shape, lambda b, i: (0, 0, 0, 0)),
        ],
        out_specs=pl.BlockSpec((None, group * QBLOCK, MIX_A), lambda b, i: (b, i, 0)),
        out_shape=jax.ShapeDtypeStruct((batch, S, MIX_A), BF16),
        compiler_params=_cparams(("parallel", "arbitrary")),
        name="attn_b",
    )(sink, seq(q), seq(k), seq(v), bias)
    return o.reshape(batch * S, MIX_A)


def _merge_work(o_refs, l_refs, oscr_ref, lscr_ref, mrg_ref):
    n_d = len(DILATIONS)
    tm = mrg_ref.shape[0]
    n_blk = MIX_A // LANE

    def unpermute(di, d, r):
        rows = slice(None) if d == 1 else pl.ds(r, tm // d, stride=d)
        lscr_ref[di, rows, :] = l_refs[di][:, r * LANE:(r + 1) * LANE]
        for c in range(n_blk):
            src = o_refs[di][:, r * MIX_A + c * LANE:r * MIX_A + (c + 1) * LANE]
            oscr_ref[di * n_blk + c, rows, :] = src.astype(F32)

    def weights():
        lse = [lscr_ref[di] for di in range(n_d)]
        mx = functools.reduce(jnp.maximum, lse)
        ex = [jnp.exp2(l - mx) for l in lse]
        inv = 1.0 / functools.reduce(jnp.add, ex)
        for di, e in enumerate(ex):
            lscr_ref[di] = e * inv

    def merge(c):
        lane = lax.broadcasted_iota(jnp.int32, (tm, LANE), 1)
        lo, hi = 2 * c * LSE_LANES, (2 * c + 1) * LSE_LANES
        merged = jnp.zeros((tm, LANE), F32)
        for di in range(n_d):
            wt = lscr_ref[di]
            wfull = jnp.where(lane < HEAD_DIM, wt[:, lo:lo + 1], wt[:, hi:hi + 1])
            merged = merged + wfull * oscr_ref[di * n_blk + c]
        mrg_ref[:, c * LANE:(c + 1) * LANE] = merged.astype(BF16)

    work = [functools.partial(unpermute, di, d, r)
            for di, d in enumerate(DILATIONS) for r in range(d)]
    return work + [weights] + [functools.partial(merge, c) for c in range(n_blk)]


def _mix_ffn_ple_kernel(x_ref, *refs):
    n_d = len(DILATIONS)
    o_refs, l_refs = refs[:n_d], refs[n_d:2 * n_d]
    (ob_ref, p_ref, wo_ref, g2_ref, win_ref, wout_ref, gp_ref, wg_ref, wp_ref, out_ref,
     x2_carry, h_carry, x2_work, h_work, act_ref, oscr_ref, lscr_ref, mrg_ref) = refs[2 * n_d:]

    @pl.when(pl.program_id(0) == 0)
    def _():
        x2_carry[...] = jnp.zeros_like(x2_carry)
        h_carry[...] = jnp.zeros_like(h_carry)

    x2_work[...] = x2_carry[...]
    h_work[...] = h_carry[...]

    merge_work = _merge_work(o_refs, l_refs, oscr_ref, lscr_ref, mrg_ref)
    x3 = x2_work[...] + 0.5 * _swiglu(h_work[...], win_ref, wout_ref, act_ref, merge_work)
    hp = (x3 * _rms_scale(x3) * gp_ref[...]).astype(BF16)

    x2 = (x_ref[...]
          + jnp.dot(mrg_ref[...], wo_ref[:MIX_A, :], preferred_element_type=F32)
          + jnp.dot(ob_ref[...], wo_ref[MIX_A:, :], preferred_element_type=F32))
    x2_carry[...] = x2
    h_carry[...] = (x2 * _rms_scale(x2) * g2_ref[...]).astype(BF16)

    gate = jax.nn.sigmoid(jnp.dot(hp, wg_ref[...], preferred_element_type=F32))
    proj = jnp.dot(p_ref[...].astype(BF16), wp_ref[...], preferred_element_type=F32)
    out_ref[...] = x3 + gate * proj


def _mix_ffn_ple(x, outs, lses, ob, p, w_o, g2, w_in, w_out, gp, w_gate, w_proj, layer,
                 *, tm=ROW_TILE):
    T, D = x.shape
    nt = T // tm
    d_ff = w_out.shape[1]
    n_d = len(DILATIONS)
    cur = lambda i: (jnp.minimum(i, nt - 1), 0)
    prev = lambda i: (jnp.maximum(i - 1, 0), 0)
    gain = lambda: _resident((None, 1, D), lambda i: (layer, 0, 0))
    return pl.pallas_call(
        _mix_ffn_ple_kernel,
        grid=(nt + 1,),
        in_specs=([pl.BlockSpec((tm, D), cur)]
                  + [pl.BlockSpec((tm // d, d * MIX_A), cur) for d in DILATIONS]
                  + [pl.BlockSpec((tm // d, d * LANE), cur) for d in DILATIONS]
                  + [pl.BlockSpec((tm, MIX_A), cur),
                     pl.BlockSpec((None, tm, p.shape[-1]),
                                  lambda i: (layer, jnp.maximum(i - 1, 0), 0)),
                     _layer_block(w_o, layer), gain(), _layer_block(w_in, layer),
                     _layer_block(w_out, layer), gain(), _layer_block(w_gate, layer),
                     _layer_block(w_proj, layer)]),
        out_specs=pl.BlockSpec((tm, D), prev),
        out_shape=jax.ShapeDtypeStruct((T, D), F32),
        scratch_shapes=[pltpu.VMEM((tm, D), F32), pltpu.VMEM((tm, D), BF16),
                        pltpu.VMEM((tm, D), F32), pltpu.VMEM((tm, D), BF16),
                        pltpu.VMEM((tm, d_ff), BF16),
                        pltpu.VMEM((n_d * MIX_A // LANE, tm, LANE), F32),
                        pltpu.VMEM((n_d, tm, LANE), F32),
                        pltpu.VMEM((tm, MIX_A), BF16)],
        compiler_params=_cparams(("arbitrary",)),
        name="mix_ffn_ple",
    )(x, *outs, *lses, ob, p, w_o, g2, w_in, w_out, gp, w_gate, w_proj)


def kernel(x, p, rel_bias, norm_ffn1, ffn1_w_in, ffn1_w_out, norm_mix, w_qkv, q_norm_a, k_norm_a, q_norm_b, k_norm_b, sink_b, w_o, norm_ffn2, ffn2_w_in, ffn2_w_out, norm_ple, w_ple_gate, w_ple_proj):
    B, S, D = x.shape
    depth = p.shape[0]
    T = B * S
    x = x.reshape(T, D)
    p = p.reshape(depth, T, p.shape[-1])

    bias_a = [_bias_tiles(rel_bias, window // (2 * d), d, 0) for window, d in DILATED_CONFIGS]
    bias_b = _bias_tiles(rel_bias, SWA_RADIUS, 1, N_HEADS)

    scale = HEAD_DIM ** -0.5 * LOG2E
    tile = lambda g, n: jnp.tile(g, (1, n))
    gains = jnp.concatenate([
        tile(q_norm_a * scale, N_HEADS), tile(k_norm_a, N_HEADS),
        jnp.ones((depth, MIX_A), F32),
        tile(q_norm_b * scale, N_HEADS), tile(k_norm_b, N_KV_B),
        jnp.ones((depth, KV_B), F32)], axis=1)[:, None]
    row = lambda g: g[:, None]
    bf = lambda w: w.astype(BF16)
    w_in1, w_out1, w_in2, w_out2 = bf(ffn1_w_in), bf(ffn1_w_out), bf(ffn2_w_in), bf(ffn2_w_out)
    w_qkv_b, w_o_b = bf(w_qkv), bf(w_o)
    w_gate_b, w_proj_b = bf(w_ple_gate), bf(w_ple_proj)

    for i in range(depth):
        x = _ffn(x, row(norm_ffn1), w_in1, w_out1, i)
        (qa, ka, va), (qb, kb, vb) = _qkv(x, row(norm_mix), w_qkv_b, gains, i)
        outs, lses = [], []
        for di, (window, d) in enumerate(DILATED_CONFIGS):
            o, lse = _attn_a(qa[di], ka[di], va[di], bias_a[di], B, window=window, dilation=d)
            outs.append(o)
            lses.append(lse)
        ob = _attn_b(qb, kb, vb, bias_b, sink_b[i], B)
        x = _mix_ffn_ple(x, outs, lses, ob, p, w_o_b, row(norm_ffn2), w_in2, w_out2,
                         row(norm_ple), w_gate_b, w_proj_b, i)
    return x.reshape(B, S, D)
```

```python
import functools
import math

import jax
import jax.numpy as jnp
import numpy as np
from jax import lax
from jax.experimental import pallas as pl
from jax.experimental.pallas import tpu as pltpu

HEAD_DIM = 64
N_HEADS = 8
N_KV_B = 2
GROUP_B = N_HEADS // N_KV_B
MIX_A = N_HEADS * HEAD_DIM
KV_B = N_KV_B * HEAD_DIM
DILATED_CONFIGS = ((128, 1), (512, 4), (2048, 16))
DILATIONS = tuple(d for _, d in DILATED_CONFIGS)
SWA_RADIUS = 128
N_BUCKETS = 32
MAX_DISTANCE = 1024
EPS = 1e-6
NEG = -1e30
LOG2E = math.log2(math.e)
QBLOCK = 128
ATTN_GROUP = 8
LSE_LANES = 16
LANE = 128
FF_CHUNK = 256
ROW_TILE = 512

VMEM_LIMIT = 56 * 1024 * 1024

BF16 = jnp.bfloat16
F32 = jnp.float32


def _cparams(sem):
    return pltpu.CompilerParams(dimension_semantics=sem, vmem_limit_bytes=VMEM_LIMIT)


def _resident(shape, index_map):
    return pl.BlockSpec(shape, index_map, pipeline_mode=pl.Buffered(1))


def _layer_block(w, layer):
    return _resident((None,) + w.shape[1:], lambda i: (layer, 0, 0))


def _rms_scale(x):
    return lax.rsqrt(jnp.mean(x * x, axis=-1, keepdims=True) + EPS)


def _ffn_hidden(h, win_ref, act_ref, fillers=()):
    d_ff = act_ref.shape[-1]
    chunks = range(0, d_ff, FF_CHUNK)
    fillers = list(fillers)
    per_chunk = -(-len(fillers) // len(chunks))
    for c in chunks:
        gate = jnp.dot(h, win_ref[:, c:c + FF_CHUNK], preferred_element_type=F32)
        up = jnp.dot(h, win_ref[:, d_ff + c:d_ff + c + FF_CHUNK], preferred_element_type=F32)
        act_ref[:, c:c + FF_CHUNK] = (gate * jax.nn.sigmoid(gate) * up).astype(BF16)
        for fill in fillers[:per_chunk]:
            fill()
        del fillers[:per_chunk]


def _ffn_down(act_ref, wout_ref):
    return jnp.dot(act_ref[...], wout_ref[...], preferred_element_type=F32)


def _head_mean_sq(y):
    sq = y * y
    lo_half = lax.broadcasted_iota(jnp.int32, sq.shape, 1) < HEAD_DIM
    s_lo = jnp.sum(jnp.where(lo_half, sq, 0.0), axis=-1, keepdims=True)
    s_hi = jnp.sum(jnp.where(lo_half, 0.0, sq), axis=-1, keepdims=True)
    return jnp.where(lo_half, s_lo, s_hi) * (1.0 / HEAD_DIM)


def _qk_normed(y, gain):
    return y * lax.rsqrt(_head_mean_sq(y) + EPS) * gain


def _qkv_work(x_ref, g_ref, w_ref, gq_ref, a_refs, qb_ref, kb_ref, vb_ref, h_ref, scr_ref):
    n_d = len(DILATIONS)
    tm = x_ref.shape[0]
    n_blk = MIX_A // LANE

    def normalise():
        x = x_ref[...]
        h_ref[...] = (x * _rms_scale(x) * g_ref[...]).astype(BF16)

    def project_a(gi, normed):
        col = gi * MIX_A
        y = jnp.dot(h_ref[...], w_ref[:, col:col + MIX_A], preferred_element_type=F32)
        for c in range(n_blk):
            yc = y[:, c * LANE:(c + 1) * LANE]
            if normed:
                yc = _qk_normed(yc, gq_ref[:, col + c * LANE:col + (c + 1) * LANE])
            scr_ref[gi * n_blk + c] = yc

    def store_a(gi, di, d):
        out = a_refs[gi * n_d + di]
        for r in range(d):
            for c in range(n_blk):
                slab = scr_ref.at[gi * n_blk + c]
                rows = slab[...] if d == 1 else slab[pl.ds(r, tm // d, stride=d), :]
                out[:, r * MIX_A + c * LANE:r * MIX_A + (c + 1) * LANE] = rows.astype(BF16)

    def project_qb():
        col = 3 * MIX_A
        yq = jnp.dot(h_ref[...], w_ref[:, col:col + MIX_A], preferred_element_type=F32)
        for c in range(n_blk):
            cols = slice(c * LANE, (c + 1) * LANE)
            gain = gq_ref[:, col + c * LANE:col + (c + 1) * LANE]
            qb_ref[:, cols] = _qk_normed(yq[:, cols], gain).astype(BF16)

    def project_kvb():
        col = 4 * MIX_A
        ykv = jnp.dot(h_ref[...], w_ref[:, col:col + 2 * KV_B], preferred_element_type=F32)
        lane = lax.broadcasted_iota(jnp.int32, (tm, LANE), 1)
        for ref, t in ((kb_ref, _qk_normed(ykv[:, :KV_B], gq_ref[:, col:col + KV_B])),
                       (vb_ref, ykv[:, KV_B:])):
            swapped = pltpu.roll(t, HEAD_DIM, 1)
            ref[:, :LANE] = jnp.where(lane < HEAD_DIM, t, swapped).astype(BF16)
            ref[:, LANE:] = jnp.where(lane < HEAD_DIM, swapped, t).astype(BF16)

    work = [normalise]
    for gi, normed in enumerate((True, True, False)):
        work.append(functools.partial(project_a, gi, normed))
        work += [functools.partial(store_a, gi, di, d) for di, d in enumerate(DILATIONS)]
    return work + [project_qb, project_kvb]


def _ffn_qkv_kernel(x_ref, g1_ref, win_ref, wout_ref, gm_ref, wqkv_ref, gq_ref, x1_ref, *refs):
    n_d = len(DILATIONS)
    a_refs = refs[:3 * n_d]
    qb_ref, kb_ref, vb_ref, x1_carry, x1_work, act_ref, hm_ref, scr_ref = refs[3 * n_d:]

    @pl.when(pl.program_id(0) == 0)
    def _():
        x1_carry[...] = jnp.zeros_like(x1_carry)

    x1_work[...] = x1_carry[...]

    x = x_ref[...]
    qkv_work = _qkv_work(x1_work, gm_ref, wqkv_ref, gq_ref, a_refs, qb_ref, kb_ref, vb_ref,
                         hm_ref, scr_ref)
    _ffn_hidden((x * _rms_scale(x) * g1_ref[...]).astype(BF16), win_ref, act_ref, qkv_work)
    x1 = x_ref[...] + 0.5 * _ffn_down(act_ref, wout_ref)
    x1_ref[...] = x1
    x1_carry[...] = x1


def _ffn_qkv(x, g1, w_in, w_out, gm, w_qkv, gains, layer, *, tm=ROW_TILE):
    T, D = x.shape
    nt = T // tm
    d_ff = w_out.shape[1]
    cur = lambda i: (jnp.minimum(i, nt - 1), 0)
    prev = lambda i: (jnp.maximum(i - 1, 0), 0)
    gain = lambda width: _resident((None, 1, width), lambda i: (layer, 0, 0))
    a_shapes = [(d, MIX_A) for _ in range(3) for d in DILATIONS]
    b_widths = [MIX_A, N_KV_B * LANE, N_KV_B * LANE]
    shapes = [(T // d, d * c) for d, c in a_shapes] + [(T, c) for c in b_widths]
    blocks = [(tm // d, d * c) for d, c in a_shapes] + [(tm, c) for c in b_widths]
    outs = pl.pallas_call(
        _ffn_qkv_kernel,
        grid=(nt + 1,),
        in_specs=[
            pl.BlockSpec((tm, D), cur), gain(D), _layer_block(w_in, layer),
            _layer_block(w_out, layer), gain(D), _layer_block(w_qkv, layer),
            gain(gains.shape[-1]),
        ],
        out_specs=[pl.BlockSpec((tm, D), cur)] + [pl.BlockSpec(b, prev) for b in blocks],
        out_shape=([jax.ShapeDtypeStruct((T, D), F32)]
                   + [jax.ShapeDtypeStruct(s, BF16) for s in shapes]),
        scratch_shapes=[pltpu.VMEM((tm, D), F32), pltpu.VMEM((tm, D), F32),
                        pltpu.VMEM((tm, d_ff), BF16), pltpu.VMEM((tm, D), BF16),
                        pltpu.VMEM((3 * MIX_A // LANE, tm, LANE), F32)],
        compiler_params=_cparams(("arbitrary",)),
        name="ffn_qkv",
    )(x, g1, w_in, w_out, gm, w_qkv, gains)
    n_d = len(DILATIONS)
    x1, outs = outs[0], outs[1:]
    return x1, (outs[:n_d], outs[n_d:2 * n_d], outs[2 * n_d:3 * n_d]), outs[3 * n_d:]


def _t5_bucket_np(rel):
    half = N_BUCKETS // 2
    max_exact = half // 2
    ret = np.where(rel > 0, half, 0)
    n = np.abs(rel)
    nf = np.maximum(n, 1).astype(np.float64)
    large = max_exact + (np.log(nf / max_exact) / math.log(MAX_DISTANCE / max_exact)
                         * (half - max_exact)).astype(np.int64)
    large = np.minimum(large, half - 1)
    return ret + np.where(n < max_exact, n, large)


def _bucket_tiles(radius, dilation):
    w = QBLOCK + 2 * radius
    key = np.arange(w)[:, None]
    q = np.arange(QBLOCK)[None, :]
    tiles = []
    for off in (0, -radius, -2 * radius):
        rel = off + key - q
        tiles.append(np.where(np.abs(rel) <= radius, _t5_bucket_np(rel * dilation), -1))
    return np.stack(tiles).astype(np.int32)


def _bias_kernel(rb_ref, idx_ref, o_ref, *, head0, buckets):
    idx = idx_ref[...]
    for h in range(N_HEADS):
        tile = jnp.full(idx.shape, NEG, F32)
        for b in buckets:
            tile = jnp.where(idx == b, rb_ref[b, head0 + h] * LOG2E, tile)
        o_ref[h] = tile


def _bias_tiles(rel_bias, radius, dilation, head0):
    idx_np = _bucket_tiles(radius, dilation)
    buckets = tuple(int(b) for b in np.unique(idx_np[idx_np >= 0]))
    nv, w, bq = idx_np.shape
    return pl.pallas_call(
        functools.partial(_bias_kernel, head0=head0, buckets=buckets),
        grid=(nv,),
        in_specs=[
            pl.BlockSpec(memory_space=pltpu.SMEM),
            pl.BlockSpec((None, w, bq), lambda v: (v, 0, 0)),
        ],
        out_specs=pl.BlockSpec((None, N_HEADS, w, bq), lambda v: (v, 0, 0, 0)),
        out_shape=jax.ShapeDtypeStruct((nv, N_HEADS, w, bq), F32),
        compiler_params=_cparams(("arbitrary",)),
        name="bias_tiles",
    )(rel_bias, jnp.asarray(idx_np))


def _window(i, nb, radius, length):
    w = QBLOCK + 2 * radius
    start = pl.multiple_of(jnp.clip(i * QBLOCK - radius, 0, length - w), radius)
    var = jnp.where(i == 0, 0, jnp.where(i == nb - 1, 2, 1))
    return start, var


_NT = (((1,), (1,)), ((), ()))
_TN = (((0,), (0,)), ((), ()))


def _head_softmax(kp, qh, vp, bias, sink=None):
    s = lax.dot_general(kp, qh, _NT, preferred_element_type=F32) + bias
    m = jnp.max(s, axis=0, keepdims=True)
    if sink is not None:
        m = jnp.maximum(m, sink)
    p = jnp.exp2(s - m)
    den = jnp.sum(p, axis=0, keepdims=True)
    if sink is not None:
        den = den + jnp.exp2(sink - m)
    o = lax.dot_general(vp, p.astype(BF16), _TN, preferred_element_type=F32)
    return o * (1.0 / den), m + jnp.log2(den)


def _pair_attention(qp, kp, vp, bias_lo, bias_hi, sink_lo=None, sink_hi=None):
    lane = lax.broadcasted_iota(jnp.int32, qp.shape, 1)
    zero = jnp.zeros_like(qp)
    o_lo, lse_lo = _head_softmax(kp, jnp.where(lane < HEAD_DIM, qp, zero), vp, bias_lo, sink_lo)
    o_hi, lse_hi = _head_softmax(kp, jnp.where(lane >= HEAD_DIM, qp, zero), vp, bias_hi, sink_hi)
    o_t = jnp.concatenate([o_lo[:HEAD_DIM], o_hi[HEAD_DIM:]], axis=0)
    return o_t.T, lse_lo, lse_hi


def _attn_a_kernel(q_ref, k_ref, v_ref, bias_ref, o_ref, lse_ref, *,
                   radius, length, nb, group, classes):
    w = QBLOCK + 2 * radius
    for cl, g in [(cl, g) for cl in range(classes) for g in range(group)]:
        start, var = _window(pl.program_id(2) * group + g, nb, radius, length)
        rows = slice(g * QBLOCK, (g + 1) * QBLOCK)
        lses = []
        for pr in range(N_HEADS // 2):
            cols = slice(cl * MIX_A + pr * LANE, cl * MIX_A + (pr + 1) * LANE)
            o, lse_lo, lse_hi = _pair_attention(
                q_ref[rows, cols], k_ref[pl.ds(start, w), cols], v_ref[pl.ds(start, w), cols],
                bias_ref[var, 2 * pr], bias_ref[var, 2 * pr + 1])
            o_ref[rows, cols] = o.astype(BF16)
            lses += [lse_lo, lse_hi]
        lse_t = jnp.concatenate([jnp.broadcast_to(l, (LSE_LANES, QBLOCK)) for l in lses], axis=0)
        lse_ref[rows, cl * LANE:(cl + 1) * LANE] = lse_t.T


def _attn_a(q, k, v, bias, batch, *, window, dilation):
    d = dilation
    L = q.shape[0] // batch
    nb = L // QBLOCK
    radius = window // (2 * d)
    group = min(nb, ATTN_GROUP)
    classes = min(d, ATTN_GROUP // group)
    seq = lambda t: t.reshape(batch, L, t.shape[-1])
    o, lse = pl.pallas_call(
        functools.partial(_attn_a_kernel, radius=radius, length=L, nb=nb, group=group,
                          classes=classes),
        grid=(batch, d // classes, nb // group),
        in_specs=[
            pl.BlockSpec((None, group * QBLOCK, classes * MIX_A), lambda b, r, i: (b, i, r)),
            pl.BlockSpec((None, L, classes * MIX_A), lambda b, r, i: (b, 0, r)),
            pl.BlockSpec((None, L, classes * MIX_A), lambda b, r, i: (b, 0, r)),
            _resident(bias.shape, lambda b, r, i: (0, 0, 0, 0)),
        ],
        out_specs=[
            pl.BlockSpec((None, group * QBLOCK, classes * MIX_A), lambda b, r, i: (b, i, r)),
            pl.BlockSpec((None, group * QBLOCK, classes * LANE), lambda b, r, i: (b, i, r)),
        ],
        out_shape=[
            jax.ShapeDtypeStruct((batch, L, d * MIX_A), BF16),
            jax.ShapeDtypeStruct((batch, L, d * LANE), F32),
        ],
        compiler_params=_cparams(("parallel", "parallel", "arbitrary")),
        name=f"attn_a_d{d}",
    )(seq(q), seq(k), seq(v), bias)
    return o.reshape(batch * L, d * MIX_A), lse.reshape(batch * L, d * LANE)


def _attn_b_kernel(sink_ref, q_ref, k_ref, v_ref, bias_ref, o_ref, *, radius, length, nb, group):
    w = QBLOCK + 2 * radius
    for g in range(group):
        start, var = _window(pl.program_id(1) * group + g, nb, radius, length)
        rows = slice(g * QBLOCK, (g + 1) * QBLOCK)
        for pr in range(N_HEADS // 2):
            cols = slice(pr * LANE, (pr + 1) * LANE)
            kv_cols = slice(2 * pr // GROUP_B * LANE, (2 * pr // GROUP_B + 1) * LANE)
            lo, hi = 2 * pr, 2 * pr + 1
            o, _, _ = _pair_attention(
                q_ref[rows, cols], k_ref[pl.ds(start, w), kv_cols], v_ref[pl.ds(start, w), kv_cols],
                bias_ref[var, lo], bias_ref[var, hi], sink_ref[lo] * LOG2E, sink_ref[hi] * LOG2E)
            o_ref[rows, cols] = o.astype(BF16)


def _attn_b(q, k, v, bias, sink, batch):
    S = q.shape[0] // batch
    nb = S // QBLOCK
    group = min(nb, ATTN_GROUP)
    seq = lambda t: t.reshape(batch, S, t.shape[-1])
    o = pl.pallas_call(
        functools.partial(_attn_b_kernel, radius=SWA_RADIUS, length=S, nb=nb, group=group),
        grid=(batch, nb // group),
        in_specs=[
            pl.BlockSpec(memory_space=pltpu.SMEM),
            pl.BlockSpec((None, group * QBLOCK, MIX_A), lambda b, i: (b, i, 0)),
            pl.BlockSpec((None, S, N_KV_B * LANE), lambda b, i: (b, 0, 0)),
            pl.BlockSpec((None, S, N_KV_B * LANE), lambda b, i: (b, 0, 0)),
            _resident(bias.shape, lambda b, i: (0, 0, 0, 0)),
        ],
        out_specs=pl.BlockSpec((None, group * QBLOCK, MIX_A), lambda b, i: (b, i, 0)),
        out_shape=jax.ShapeDtypeStruct((batch, S, MIX_A), BF16),
        compiler_params=_cparams(("parallel", "arbitrary")),
        name="attn_b",
    )(sink, seq(q), seq(k), seq(v), bias)
    return o.reshape(batch * S, MIX_A)


def _merge_work(o_refs, l_refs, oscr_ref, lscr_ref, mrg_ref):
    n_d = len(DILATIONS)
    tm = mrg_ref.shape[0]
    n_blk = MIX_A // LANE

    def unpermute(di, d, r):
        rows = slice(None) if d == 1 else pl.ds(r, tm // d, stride=d)
        lscr_ref[di, rows, :] = l_refs[di][:, r * LANE:(r + 1) * LANE]
        for c in range(n_blk):
            src = o_refs[di][:, r * MIX_A + c * LANE:r * MIX_A + (c + 1) * LANE]
            oscr_ref[di * n_blk + c, rows, :] = src.astype(F32)

    def weights():
        lse = [lscr_ref[di] for di in range(n_d)]
        mx = functools.reduce(jnp.maximum, lse)
        ex = [jnp.exp2(l - mx) for l in lse]
        inv = 1.0 / functools.reduce(jnp.add, ex)
        for di, e in enumerate(ex):
            lscr_ref[di] = e * inv

    def merge(c):
        lane = lax.broadcasted_iota(jnp.int32, (tm, LANE), 1)
        lo, hi = 2 * c * LSE_LANES, (2 * c + 1) * LSE_LANES
        merged = jnp.zeros((tm, LANE), F32)
        for di in range(n_d):
            wt = lscr_ref[di]
            wfull = jnp.where(lane < HEAD_DIM, wt[:, lo:lo + 1], wt[:, hi:hi + 1])
            merged = merged + wfull * oscr_ref[di * n_blk + c]
        mrg_ref[:, c * LANE:(c + 1) * LANE] = merged.astype(BF16)

    work = [functools.partial(unpermute, di, d, r)
            for di, d in enumerate(DILATIONS) for r in range(d)]
    return work + [weights] + [functools.partial(merge, c) for c in range(n_blk)]


def _mix_ffn_ple_kernel(x_ref, *refs):
    n_d = len(DILATIONS)
    o_refs, l_refs = refs[:n_d], refs[n_d:2 * n_d]
    (ob_ref, p_ref, wo_ref, g2_ref, win_ref, wout_ref, gp_ref, wg_ref, wp_ref, out_ref,
     x2_carry, h_carry, x2_work, h_work, act_ref, oscr_ref, lscr_ref, mrg_ref) = refs[2 * n_d:]

    @pl.when(pl.program_id(0) == 0)
    def _():
        x2_carry[...] = jnp.zeros_like(x2_carry)
        h_carry[...] = jnp.zeros_like(h_carry)

    x2_work[...] = x2_carry[...]
    h_work[...] = h_carry[...]

    merge_work = _merge_work(o_refs, l_refs, oscr_ref, lscr_ref, mrg_ref)
    _ffn_hidden(h_work[...], win_ref, act_ref, merge_work)
    x3 = x2_work[...] + 0.5 * _ffn_down(act_ref, wout_ref)
    hp = (x3 * _rms_scale(x3) * gp_ref[...]).astype(BF16)

    x2 = (x_ref[...]
          + jnp.dot(mrg_ref[...], wo_ref[:MIX_A, :], preferred_element_type=F32)
          + jnp.dot(ob_ref[...], wo_ref[MIX_A:, :], preferred_element_type=F32))
    x2_carry[...] = x2
    h_carry[...] = (x2 * _rms_scale(x2) * g2_ref[...]).astype(BF16)

    gate = jax.nn.sigmoid(jnp.dot(hp, wg_ref[...], preferred_element_type=F32))
    proj = jnp.dot(p_ref[...].astype(BF16), wp_ref[...], preferred_element_type=F32)
    out_ref[...] = x3 + gate * proj


def _mix_ffn_ple(x, outs, lses, ob, p, w_o, g2, w_in, w_out, gp, w_gate, w_proj, layer,
                 *, tm=ROW_TILE):
    T, D = x.shape
    nt = T // tm
    d_ff = w_out.shape[1]
    n_d = len(DILATIONS)
    cur = lambda i: (jnp.minimum(i, nt - 1), 0)
    prev = lambda i: (jnp.maximum(i - 1, 0), 0)
    gain = lambda: _resident((None, 1, D), lambda i: (layer, 0, 0))
    return pl.pallas_call(
        _mix_ffn_ple_kernel,
        grid=(nt + 1,),
        in_specs=([pl.BlockSpec((tm, D), cur)]
                  + [pl.BlockSpec((tm // d, d * MIX_A), cur) for d in DILATIONS]
                  + [pl.BlockSpec((tm // d, d * LANE), cur) for d in DILATIONS]
                  + [pl.BlockSpec((tm, MIX_A), cur),
                     pl.BlockSpec((None, tm, p.shape[-1]),
                                  lambda i: (layer, jnp.maximum(i - 1, 0), 0)),
                     _layer_block(w_o, layer), gain(), _layer_block(w_in, layer),
                     _layer_block(w_out, layer), gain(), _layer_block(w_gate, layer),
                     _layer_block(w_proj, layer)]),
        out_specs=pl.BlockSpec((tm, D), prev),
        out_shape=jax.ShapeDtypeStruct((T, D), F32),
        scratch_shapes=[pltpu.VMEM((tm, D), F32), pltpu.VMEM((tm, D), BF16),
                        pltpu.VMEM((tm, D), F32), pltpu.VMEM((tm, D), BF16),
                        pltpu.VMEM((tm, d_ff), BF16),
                        pltpu.VMEM((n_d * MIX_A // LANE, tm, LANE), F32),
                        pltpu.VMEM((n_d, tm, LANE), F32),
                        pltpu.VMEM((tm, MIX_A), BF16)],
        compiler_params=_cparams(("arbitrary",)),
        name="mix_ffn_ple",
    )(x, *outs, *lses, ob, p, w_o, g2, w_in, w_out, gp, w_gate, w_proj)


def kernel(x, p, rel_bias, norm_ffn1, ffn1_w_in, ffn1_w_out, norm_mix, w_qkv, q_norm_a, k_norm_a, q_norm_b, k_norm_b, sink_b, w_o, norm_ffn2, ffn2_w_in, ffn2_w_out, norm_ple, w_ple_gate, w_ple_proj):
    B, S, D = x.shape
    depth = p.shape[0]
    T = B * S
    x = x.reshape(T, D)
    p = p.reshape(depth, T, p.shape[-1])

    bias_a = [_bias_tiles(rel_bias, window // (2 * d), d, 0) for window, d in DILATED_CONFIGS]
    bias_b = _bias_tiles(rel_bias, SWA_RADIUS, 1, N_HEADS)

    scale = HEAD_DIM ** -0.5 * LOG2E
    tile = lambda g, n: jnp.tile(g, (1, n))
    gains = jnp.concatenate([
        tile(q_norm_a * scale, N_HEADS), tile(k_norm_a, N_HEADS),
        jnp.ones((depth, MIX_A), F32),
        tile(q_norm_b * scale, N_HEADS), tile(k_norm_b, N_KV_B),
        jnp.ones((depth, KV_B), F32)], axis=1)[:, None]
    row = lambda g: g[:, None]
    bf = lambda w: w.astype(BF16)
    w_in1, w_out1, w_in2, w_out2 = bf(ffn1_w_in), bf(ffn1_w_out), bf(ffn2_w_in), bf(ffn2_w_out)
    w_qkv_b, w_o_b = bf(w_qkv), bf(w_o)
    w_gate_b, w_proj_b = bf(w_ple_gate), bf(w_ple_proj)

    for i in range(depth):
        x, (qa, ka, va), (qb, kb, vb) = _ffn_qkv(x, row(norm_ffn1), w_in1, w_out1,
                                                 row(norm_mix), w_qkv_b, gains, i)
        outs, lses = [], []
        for di, (window, d) in enumerate(DILATED_CONFIGS):
            o, lse = _attn_a(qa[di], ka[di], va[di], bias_a[di], B, window=window, dilation=d)
            outs.append(o)
            lses.append(lse)
        ob = _attn_b(qb, kb, vb, bias_b, sink_b[i], B)
        x = _mix_ffn_ple(x, outs, lses, ob, p, w_o_b, row(norm_ffn2), w_in2, w_out2,
                         row(norm_ple), w_gate_b, w_proj_b, i)
    return x.reshape(B, S, D)
```

```python
import functools
import math

import jax
import jax.numpy as jnp
import numpy as np
from jax import lax
from jax.experimental import pallas as pl
from jax.experimental.pallas import tpu as pltpu

HEAD_DIM = 64
N_HEADS = 8
N_KV_B = 2
GROUP_B = N_HEADS // N_KV_B
MIX_A = N_HEADS * HEAD_DIM
KV_B = N_KV_B * HEAD_DIM
DILATED_CONFIGS = ((128, 1), (512, 4), (2048, 16))
DILATIONS = tuple(d for _, d in DILATED_CONFIGS)
SWA_RADIUS = 128
N_BUCKETS = 32
MAX_DISTANCE = 1024
EPS = 1e-6
NEG = -1e30
LOG2E = math.log2(math.e)
QBLOCK = 128
ATTN_GROUP = 8
LSE_LANES = 16
LANE = 128
FF_CHUNK = 256
ROW_TILE = 512

VMEM_LIMIT = 56 * 1024 * 1024

BF16 = jnp.bfloat16
F32 = jnp.float32


def _cparams(sem):
    return pltpu.CompilerParams(dimension_semantics=sem, vmem_limit_bytes=VMEM_LIMIT)


def _resident(shape, index_map):
    return pl.BlockSpec(shape, index_map, pipeline_mode=pl.Buffered(1))


def _layer_block(w, layer):
    return _resident((None,) + w.shape[1:], lambda i: (layer, 0, 0))


def _rms_scale(x):
    return lax.rsqrt(jnp.mean(x * x, axis=-1, keepdims=True) + EPS)


def _ffn_hidden(h, win_ref, act_ref, fillers=()):
    d_ff = act_ref.shape[-1]
    chunks = range(0, d_ff, FF_CHUNK)
    fillers = list(fillers)
    per_chunk = -(-len(fillers) // len(chunks))
    for c in chunks:
        e = min(c + FF_CHUNK, d_ff)
        gate = jnp.dot(h, win_ref[:, c:e], preferred_element_type=F32)
        up = jnp.dot(h, win_ref[:, d_ff + c:d_ff + e], preferred_element_type=F32)
        act_ref[:, c:e] = (gate * jax.nn.sigmoid(gate) * up).astype(BF16)
        for fill in fillers[:per_chunk]:
            fill()
        del fillers[:per_chunk]


def _ffn_down(act_ref, wout_ref):
    return jnp.dot(act_ref[...], wout_ref[...], preferred_element_type=F32)


def _head_mean_sq(y):
    sq = y * y
    lo_half = lax.broadcasted_iota(jnp.int32, sq.shape, 1) < HEAD_DIM
    s_lo = jnp.sum(jnp.where(lo_half, sq, 0.0), axis=-1, keepdims=True)
    s_hi = jnp.sum(jnp.where(lo_half, 0.0, sq), axis=-1, keepdims=True)
    return jnp.where(lo_half, s_lo, s_hi) * (1.0 / HEAD_DIM)


def _qk_normed(y, gain):
    return y * lax.rsqrt(_head_mean_sq(y) + EPS) * gain


def _qkv_work(x_ref, g_ref, w_ref, gq_ref, a_refs, qb_ref, kb_ref, vb_ref, h_ref, scr_ref):
    n_d = len(DILATIONS)
    tm = x_ref.shape[0]
    n_blk = MIX_A // LANE

    def normalise():
        x = x_ref[...]
        h_ref[...] = (x * _rms_scale(x) * g_ref[...]).astype(BF16)

    def project_a(gi, normed):
        col = gi * MIX_A
        y = jnp.dot(h_ref[...], w_ref[:, col:col + MIX_A], preferred_element_type=F32)
        for c in range(n_blk):
            yc = y[:, c * LANE:(c + 1) * LANE]
            if normed:
                yc = _qk_normed(yc, gq_ref[:, col + c * LANE:col + (c + 1) * LANE])
            scr_ref[gi * n_blk + c] = yc

    def store_a(gi, di, d):
        out = a_refs[gi * n_d + di]
        for r in range(d):
            for c in range(n_blk):
                slab = scr_ref.at[gi * n_blk + c]
                rows = slab[...] if d == 1 else slab[pl.ds(r, tm // d, stride=d), :]
                out[:, r * MIX_A + c * LANE:r * MIX_A + (c + 1) * LANE] = rows.astype(BF16)

    def project_qb():
        col = 3 * MIX_A
        yq = jnp.dot(h_ref[...], w_ref[:, col:col + MIX_A], preferred_element_type=F32)
        for c in range(n_blk):
            cols = slice(c * LANE, (c + 1) * LANE)
            gain = gq_ref[:, col + c * LANE:col + (c + 1) * LANE]
            qb_ref[:, cols] = _qk_normed(yq[:, cols], gain).astype(BF16)

    def project_kvb():
        col = 4 * MIX_A
        ykv = jnp.dot(h_ref[...], w_ref[:, col:col + 2 * KV_B], preferred_element_type=F32)
        lane = lax.broadcasted_iota(jnp.int32, (tm, LANE), 1)
        for ref, t in ((kb_ref, _qk_normed(ykv[:, :KV_B], gq_ref[:, col:col + KV_B])),
                       (vb_ref, ykv[:, KV_B:])):
            swapped = pltpu.roll(t, HEAD_DIM, 1)
            ref[:, :LANE] = jnp.where(lane < HEAD_DIM, t, swapped).astype(BF16)
            ref[:, LANE:] = jnp.where(lane < HEAD_DIM, swapped, t).astype(BF16)

    work = [normalise]
    for gi, normed in enumerate((True, True, False)):
        work.append(functools.partial(project_a, gi, normed))
        work += [functools.partial(store_a, gi, di, d) for di, d in enumerate(DILATIONS)]
    return work + [project_qb, project_kvb]


def _ffn_qkv_kernel(x_ref, g1_ref, win_ref, wout_ref, gm_ref, wqkv_ref, gq_ref, x1_ref, *refs):
    n_d = len(DILATIONS)
    a_refs = refs[:3 * n_d]
    qb_ref, kb_ref, vb_ref, x1_carry, x1_work, act_ref, hm_ref, scr_ref = refs[3 * n_d:]
    i = pl.program_id(0)
    last = pl.num_programs(0) - 1

    def qkv_work(src_ref):
        return _qkv_work(src_ref, gm_ref, wqkv_ref, gq_ref, a_refs, qb_ref, kb_ref, vb_ref,
                         hm_ref, scr_ref)

    def stage1(fillers=()):
        x = x_ref[...]
        _ffn_hidden((x * _rms_scale(x) * g1_ref[...]).astype(BF16), win_ref, act_ref, fillers)
        x1 = x_ref[...] + 0.5 * _ffn_down(act_ref, wout_ref)
        x1_ref[...] = x1
        x1_carry[...] = x1

    @pl.when(i == 0)
    def _():
        stage1()

    @pl.when(jnp.logical_and(i > 0, i < last))
    def _():
        x1_work[...] = x1_carry[...]
        stage1(qkv_work(x1_work))

    @pl.when(i == last)
    def _():
        for work in qkv_work(x1_carry):
            work()


def _ffn_qkv(x, g1, w_in, w_out, gm, w_qkv, gains, layer, *, tm=ROW_TILE):
    T, D = x.shape
    nt = T // tm
    d_ff = w_out.shape[1]
    cur = lambda i: (jnp.minimum(i, nt - 1), 0)
    prev = lambda i: (jnp.maximum(i - 1, 0), 0)
    gain = lambda width: _resident((None, 1, width), lambda i: (layer, 0, 0))
    a_shapes = [(d, MIX_A) for _ in range(3) for d in DILATIONS]
    b_widths = [MIX_A, N_KV_B * LANE, N_KV_B * LANE]
    shapes = [(T // d, d * c) for d, c in a_shapes] + [(T, c) for c in b_widths]
    blocks = [(tm // d, d * c) for d, c in a_shapes] + [(tm, c) for c in b_widths]
    outs = pl.pallas_call(
        _ffn_qkv_kernel,
        grid=(nt + 1,),
        in_specs=[
            pl.BlockSpec((tm, D), cur), gain(D), _layer_block(w_in, layer),
            _layer_block(w_out, layer), gain(D), _layer_block(w_qkv, layer),
            gain(gains.shape[-1]),
        ],
        out_specs=[pl.BlockSpec((tm, D), cur)] + [pl.BlockSpec(b, prev) for b in blocks],
        out_shape=([jax.ShapeDtypeStruct((T, D), F32)]
                   + [jax.ShapeDtypeStruct(s, BF16) for s in shapes]),
        scratch_shapes=[pltpu.VMEM((tm, D), F32), pltpu.VMEM((tm, D), F32),
                        pltpu.VMEM((tm, d_ff), BF16), pltpu.VMEM((tm, D), BF16),
                        pltpu.VMEM((3 * MIX_A // LANE, tm, LANE), F32)],
        compiler_params=_cparams(("arbitrary",)),
        name="ffn_qkv",
    )(x, g1, w_in, w_out, gm, w_qkv, gains)
    n_d = len(DILATIONS)
    x1, outs = outs[0], outs[1:]
    return x1, (outs[:n_d], outs[n_d:2 * n_d], outs[2 * n_d:3 * n_d]), outs[3 * n_d:]


def _t5_bucket_np(rel):
    half = N_BUCKETS // 2
    max_exact = half // 2
    ret = np.where(rel > 0, half, 0)
    n = np.abs(rel)
    nf = np.maximum(n, 1).astype(np.float64)
    large = max_exact + (np.log(nf / max_exact) / math.log(MAX_DISTANCE / max_exact)
                         * (half - max_exact)).astype(np.int64)
    large = np.minimum(large, half - 1)
    return ret + np.where(n < max_exact, n, large)


def _bucket_tiles(radius, dilation):
    w = QBLOCK + 2 * radius
    key = np.arange(w)[:, None]
    q = np.arange(QBLOCK)[None, :]
    tiles = []
    for off in (0, -radius, -2 * radius):
        rel = off + key - q
        tiles.append(np.where(np.abs(rel) <= radius, _t5_bucket_np(rel * dilation), -1))
    return np.stack(tiles).astype(np.int32)


def _bias_kernel(rb_ref, idx_ref, o_ref, *, head0, buckets):
    idx = idx_ref[...]
    for h in range(N_HEADS):
        tile = jnp.full(idx.shape, NEG, F32)
        for b in buckets:
            tile = jnp.where(idx == b, rb_ref[b, head0 + h] * LOG2E, tile)
        o_ref[h] = tile


def _bias_tiles(rel_bias, radius, dilation, head0):
    idx_np = _bucket_tiles(radius, dilation)
    buckets = tuple(int(b) for b in np.unique(idx_np[idx_np >= 0]))
    nv, w, bq = idx_np.shape
    return pl.pallas_call(
        functools.partial(_bias_kernel, head0=head0, buckets=buckets),
        grid=(nv,),
        in_specs=[
            pl.BlockSpec(memory_space=pltpu.SMEM),
            pl.BlockSpec((None, w, bq), lambda v: (v, 0, 0)),
        ],
        out_specs=pl.BlockSpec((None, N_HEADS, w, bq), lambda v: (v, 0, 0, 0)),
        out_shape=jax.ShapeDtypeStruct((nv, N_HEADS, w, bq), F32),
        compiler_params=_cparams(("arbitrary",)),
        name="bias_tiles",
    )(rel_bias, jnp.asarray(idx_np))


def _window(i, nb, radius, length):
    w = QBLOCK + 2 * radius
    start = pl.multiple_of(jnp.clip(i * QBLOCK - radius, 0, length - w), radius)
    var = jnp.where(i == 0, 0, jnp.where(i == nb - 1, 2, 1))
    return start, var


_NT = (((1,), (1,)), ((), ()))
_TN = (((0,), (0,)), ((), ()))


def _softmax_rows(s, sink=None):
    m = jnp.max(s, axis=0, keepdims=True)
    if sink is not None:
        m = jnp.maximum(m, sink)
    p = jnp.exp2(s - m)
    den = jnp.sum(p, axis=0, keepdims=True)
    if sink is not None:
        den = den + jnp.exp2(sink - m)
    return p.astype(BF16), 1.0 / den, m + jnp.log2(den)


def _pair_attention(qp, kp, vp, bias_lo, bias_hi, sink_lo=None, sink_hi=None):
    lane = lax.broadcasted_iota(jnp.int32, qp.shape, 1)
    zero = jnp.zeros_like(qp)
    vt = vp.T
    halves = []
    for keep, bias, sink in ((lane < HEAD_DIM, bias_lo, sink_lo), (lane >= HEAD_DIM, bias_hi, sink_hi)):
        s = lax.dot_general(kp, jnp.where(keep, qp, zero), _NT, preferred_element_type=F32)
        p, inv, lse = _softmax_rows(s + bias, sink)
        halves.append((jnp.dot(vt, p, preferred_element_type=F32) * inv, lse))
    (o_lo, lse_lo), (o_hi, lse_hi) = halves
    o_t = jnp.concatenate([o_lo[:HEAD_DIM], o_hi[HEAD_DIM:]], axis=0)
    return o_t.astype(BF16).T, lse_lo, lse_hi


def _attn_a_kernel(q_ref, k_ref, v_ref, bias_ref, o_ref, lse_ref, *,
                   radius, length, nb, group, classes):
    w = QBLOCK + 2 * radius
    for cl, g in [(cl, g) for cl in range(classes) for g in range(group)]:
        start, var = _window(pl.program_id(2) * group + g, nb, radius, length)
        rows = slice(g * QBLOCK, (g + 1) * QBLOCK)
        lses = []
        for pr in range(N_HEADS // 2):
            cols = slice(cl * MIX_A + pr * LANE, cl * MIX_A + (pr + 1) * LANE)
            o, lse_lo, lse_hi = _pair_attention(
                q_ref[rows, cols], k_ref[pl.ds(start, w), cols], v_ref[pl.ds(start, w), cols],
                bias_ref[var, 2 * pr], bias_ref[var, 2 * pr + 1])
            o_ref[rows, cols] = o
            lses += [lse_lo, lse_hi]
        lse_t = jnp.concatenate([jnp.broadcast_to(l, (LSE_LANES, QBLOCK)) for l in lses], axis=0)
        lse_ref[rows, cl * LANE:(cl + 1) * LANE] = lse_t.T


def _attn_a(q, k, v, bias, batch, *, window, dilation):
    d = dilation
    L = q.shape[0] // batch
    nb = L // QBLOCK
    radius = window // (2 * d)
    group = min(nb, ATTN_GROUP)
    classes = min(d, ATTN_GROUP // group)
    seq = lambda t: t.reshape(batch, L, t.shape[-1])
    o, lse = pl.pallas_call(
        functools.partial(_attn_a_kernel, radius=radius, length=L, nb=nb, group=group,
                          classes=classes),
        grid=(batch, d // classes, nb // group),
        in_specs=[
            pl.BlockSpec((None, group * QBLOCK, classes * MIX_A), lambda b, r, i: (b, i, r)),
            pl.BlockSpec((None, L, classes * MIX_A), lambda b, r, i: (b, 0, r)),
            pl.BlockSpec((None, L, classes * MIX_A), lambda b, r, i: (b, 0, r)),
            _resident(bias.shape, lambda b, r, i: (0, 0, 0, 0)),
        ],
        out_specs=[
            pl.BlockSpec((None, group * QBLOCK, classes * MIX_A), lambda b, r, i: (b, i, r)),
            pl.BlockSpec((None, group * QBLOCK, classes * LANE), lambda b, r, i: (b, i, r)),
        ],
        out_shape=[
            jax.ShapeDtypeStruct((batch, L, d * MIX_A), BF16),
            jax.ShapeDtypeStruct((batch, L, d * LANE), F32),
        ],
        compiler_params=_cparams(("parallel", "parallel", "arbitrary")),
        name=f"attn_a_d{d}",
    )(seq(q), seq(k), seq(v), bias)
    return o.reshape(batch * L, d * MIX_A), lse.reshape(batch * L, d * LANE)


def _attn_b_kernel(sink_ref, q_ref, k_ref, v_ref, bias_ref, o_ref, *, radius, length, nb, group):
    w = QBLOCK + 2 * radius
    for g in range(group):
        start, var = _window(pl.program_id(1) * group + g, nb, radius, length)
        rows = slice(g * QBLOCK, (g + 1) * QBLOCK)
        for pr in range(N_HEADS // 2):
            cols = slice(pr * LANE, (pr + 1) * LANE)
            kv_cols = slice(2 * pr // GROUP_B * LANE, (2 * pr // GROUP_B + 1) * LANE)
            lo, hi = 2 * pr, 2 * pr + 1
            o, _, _ = _pair_attention(
                q_ref[rows, cols], k_ref[pl.ds(start, w), kv_cols], v_ref[pl.ds(start, w), kv_cols],
                bias_ref[var, lo], bias_ref[var, hi], sink_ref[lo] * LOG2E, sink_ref[hi] * LOG2E)
            o_ref[rows, cols] = o


def _attn_b(q, k, v, bias, sink, batch):
    S = q.shape[0] // batch
    nb = S // QBLOCK
    group = min(nb, ATTN_GROUP)
    seq = lambda t: t.reshape(batch, S, t.shape[-1])
    o = pl.pallas_call(
        functools.partial(_attn_b_kernel, radius=SWA_RADIUS, length=S, nb=nb, group=group),
        grid=(batch, nb // group),
        in_specs=[
            pl.BlockSpec(memory_space=pltpu.SMEM),
            pl.BlockSpec((None, group * QBLOCK, MIX_A), lambda b, i: (b, i, 0)),
            pl.BlockSpec((None, S, N_KV_B * LANE), lambda b, i: (b, 0, 0)),
            pl.BlockSpec((None, S, N_KV_B * LANE), lambda b, i: (b, 0, 0)),
            _resident(bias.shape, lambda b, i: (0, 0, 0, 0)),
        ],
        out_specs=pl.BlockSpec((None, group * QBLOCK, MIX_A), lambda b, i: (b, i, 0)),
        out_shape=jax.ShapeDtypeStruct((batch, S, MIX_A), BF16),
        compiler_params=_cparams(("parallel", "arbitrary")),
        name="attn_b",
    )(sink, seq(q), seq(k), seq(v), bias)
    return o.reshape(batch * S, MIX_A)


def _merge_work(o_refs, l_refs, oscr_ref, lscr_ref, mrg_ref):
    n_d = len(DILATIONS)
    tm = mrg_ref.shape[0]
    n_blk = MIX_A // LANE

    def unpermute(di, d, r):
        rows = slice(None) if d == 1 else pl.ds(r, tm // d, stride=d)
        lscr_ref[di, rows, :] = l_refs[di][:, r * LANE:(r + 1) * LANE]
        for c in range(n_blk):
            src = o_refs[di][:, r * MIX_A + c * LANE:r * MIX_A + (c + 1) * LANE]
            oscr_ref[di * n_blk + c, rows, :] = src.astype(F32)

    def weights():
        lse = [lscr_ref[di] for di in range(n_d)]
        mx = functools.reduce(jnp.maximum, lse)
        ex = [jnp.exp2(l - mx) for l in lse]
        inv = 1.0 / functools.reduce(jnp.add, ex)
        for di, e in enumerate(ex):
            lscr_ref[di] = e * inv

    def merge(c):
        lane = lax.broadcasted_iota(jnp.int32, (tm, LANE), 1)
        lo, hi = 2 * c * LSE_LANES, (2 * c + 1) * LSE_LANES
        merged = jnp.zeros((tm, LANE), F32)
        for di in range(n_d):
            wt = lscr_ref[di]
            wfull = jnp.where(lane < HEAD_DIM, wt[:, lo:lo + 1], wt[:, hi:hi + 1])
            merged = merged + wfull * oscr_ref[di * n_blk + c]
        mrg_ref[:, c * LANE:(c + 1) * LANE] = merged.astype(BF16)

    work = [functools.partial(unpermute, di, d, r)
            for di, d in enumerate(DILATIONS) for r in range(d)]
    return work + [weights] + [functools.partial(merge, c) for c in range(n_blk)]


def _mix_ffn_ple_kernel(x_ref, *refs):
    n_d = len(DILATIONS)
    o_refs, l_refs = refs[:n_d], refs[n_d:2 * n_d]
    (ob_ref, p_ref, wo_ref, g2_ref, win_ref, wout_ref, gp_ref, wg_ref, wp_ref, out_ref,
     x2_carry, h_carry, x2_work, h_work, act_ref, oscr_ref, lscr_ref, mrg_ref) = refs[2 * n_d:]

    i = pl.program_id(0)
    last = pl.num_programs(0) - 1

    def merge_work():
        return _merge_work(o_refs, l_refs, oscr_ref, lscr_ref, mrg_ref)

    def project():
        x2 = (x_ref[...]
              + jnp.dot(mrg_ref[...], wo_ref[:MIX_A, :], preferred_element_type=F32)
              + jnp.dot(ob_ref[...], wo_ref[MIX_A:, :], preferred_element_type=F32))
        x2_carry[...] = x2
        h_carry[...] = (x2 * _rms_scale(x2) * g2_ref[...]).astype(BF16)

    def stage2(x2_ref, h_ref, fillers=(), between=lambda: None):
        _ffn_hidden(h_ref[...], win_ref, act_ref, fillers)
        x3 = x2_ref[...] + 0.5 * _ffn_down(act_ref, wout_ref)
        hp = (x3 * _rms_scale(x3) * gp_ref[...]).astype(BF16)
        between()
        gate = jax.nn.sigmoid(jnp.dot(hp, wg_ref[...], preferred_element_type=F32))
        proj = jnp.dot(p_ref[...].astype(BF16), wp_ref[...], preferred_element_type=F32)
        out_ref[...] = x3 + gate * proj

    @pl.when(i == 0)
    def _():
        for work in merge_work():
            work()
        project()

    @pl.when(jnp.logical_and(i > 0, i < last))
    def _():
        x2_work[...] = x2_carry[...]
        h_work[...] = h_carry[...]
        stage2(x2_work, h_work, merge_work(), project)

    @pl.when(i == last)
    def _():
        stage2(x2_carry, h_carry)


def _mix_ffn_ple(x, outs, lses, ob, p, w_o, g2, w_in, w_out, gp, w_gate, w_proj, layer,
                 *, tm=ROW_TILE):
    T, D = x.shape
    nt = T // tm
    d_ff = w_out.shape[1]
    n_d = len(DILATIONS)
    cur = lambda i: (jnp.minimum(i, nt - 1), 0)
    prev = lambda i: (jnp.maximum(i - 1, 0), 0)
    gain = lambda: _resident((None, 1, D), lambda i: (layer, 0, 0))
    return pl.pallas_call(
        _mix_ffn_ple_kernel,
        grid=(nt + 1,),
        in_specs=([pl.BlockSpec((tm, D), cur)]
                  + [pl.BlockSpec((tm // d, d * MIX_A), cur) for d in DILATIONS]
                  + [pl.BlockSpec((tm // d, d * LANE), cur) for d in DILATIONS]
                  + [pl.BlockSpec((tm, MIX_A), cur),
                     pl.BlockSpec((None, tm, p.shape[-1]),
                                  lambda i: (layer, jnp.maximum(i - 1, 0), 0)),
                     _layer_block(w_o, layer), gain(), _layer_block(w_in, layer),
                     _layer_block(w_out, layer), gain(), _layer_block(w_gate, layer),
                     _layer_block(w_proj, layer)]),
        out_specs=pl.BlockSpec((tm, D), prev),
        out_shape=jax.ShapeDtypeStruct((T, D), F32),
        scratch_shapes=[pltpu.VMEM((tm, D), F32), pltpu.VMEM((tm, D), BF16),
                        pltpu.VMEM((tm, D), F32), pltpu.VMEM((tm, D), BF16),
                        pltpu.VMEM((tm, d_ff), BF16),
                        pltpu.VMEM((n_d * MIX_A // LANE, tm, LANE), F32),
                        pltpu.VMEM((n_d, tm, LANE), F32),
                        pltpu.VMEM((tm, MIX_A), BF16)],
        compiler_params=_cparams(("arbitrary",)),
        name="mix_ffn_ple",
    )(x, *outs, *lses, ob, p, w_o, g2, w_in, w_out, gp, w_gate, w_proj)


def kernel(x, p, rel_bias, norm_ffn1, ffn1_w_in, ffn1_w_out, norm_mix, w_qkv, q_norm_a, k_norm_a, q_norm_b, k_norm_b, sink_b, w_o, norm_ffn2, ffn2_w_in, ffn2_w_out, norm_ple, w_ple_gate, w_ple_proj):
    B, S, D = x.shape
    depth = p.shape[0]
    T = B * S
    x = x.reshape(T, D)
    p = p.reshape(depth, T, p.shape[-1])

    bias_a = [_bias_tiles(rel_bias, window // (2 * d), d, 0) for window, d in DILATED_CONFIGS]
    bias_b = _bias_tiles(rel_bias, SWA_RADIUS, 1, N_HEADS)

    scale = HEAD_DIM ** -0.5 * LOG2E
    tile = lambda g, n: jnp.tile(g, (1, n))
    gains = jnp.concatenate([
        tile(q_norm_a * scale, N_HEADS), tile(k_norm_a, N_HEADS),
        jnp.ones((depth, MIX_A), F32),
        tile(q_norm_b * scale, N_HEADS), tile(k_norm_b, N_KV_B),
        jnp.ones((depth, KV_B), F32)], axis=1)[:, None]
    row = lambda g: g[:, None]
    bf = lambda w: w.astype(BF16)
    w_in1, w_out1, w_in2, w_out2 = bf(ffn1_w_in), bf(ffn1_w_out), bf(ffn2_w_in), bf(ffn2_w_out)
    w_qkv_b, w_o_b = bf(w_qkv), bf(w_o)
    w_gate_b, w_proj_b = bf(w_ple_gate), bf(w_ple_proj)

    for i in range(depth):
        x, (qa, ka, va), (qb, kb, vb) = _ffn_qkv(x, row(norm_ffn1), w_in1, w_out1,
                                                 row(norm_mix), w_qkv_b, gains, i)
        outs, lses = [], []
        for di, (window, d) in enumerate(DILATED_CONFIGS):
            o, lse = _attn_a(qa[di], ka[di], va[di], bias_a[di], B, window=window, dilation=d)
            outs.append(o)
            lses.append(lse)
        ob = _attn_b(qb, kb, vb, bias_b, sink_b[i], B)
        x = _mix_ffn_ple(x, outs, lses, ob, p, w_o_b, row(norm_ffn2), w_in2, w_out2,
                         row(norm_ple), w_gate_b, w_proj_b, i)
    return x.reshape(B, S, D)
```

```python
import functools
import math

import jax
import jax.numpy as jnp
import numpy as np
from jax import lax
from jax.experimental import pallas as pl
from jax.experimental.pallas import tpu as pltpu

HEAD_DIM = 64
N_HEADS = 8
N_KV_B = 2
GROUP_B = N_HEADS // N_KV_B
MIX_A = N_HEADS * HEAD_DIM
KV_B = N_KV_B * HEAD_DIM
DILATED_CONFIGS = ((128, 1), (512, 4), (2048, 16))
DILATIONS = tuple(d for _, d in DILATED_CONFIGS)
SWA_RADIUS = 128
N_BUCKETS = 32
MAX_DISTANCE = 1024
EPS = 1e-6
NEG = -1e30
LOG2E = math.log2(math.e)
QBLOCK = 128
ATTN_GROUP = 8
LSE_LANES = 16
LANE = 128
BF16_TILE_ROWS = 16
FF_CHUNK = 256
ROW_TILE = 512

VMEM_LIMIT = 56 * 1024 * 1024

BF16 = jnp.bfloat16
F32 = jnp.float32


def _cparams(sem):
    return pltpu.CompilerParams(dimension_semantics=sem, vmem_limit_bytes=VMEM_LIMIT)


def _resident(shape, index_map):
    return pl.BlockSpec(shape, index_map, pipeline_mode=pl.Buffered(1))


def _layer_block(w, layer):
    return _resident((None,) + w.shape[1:], lambda i: (layer, 0, 0))


def _whole(w):
    return _resident(w.shape, lambda i: (0, 0))


def _cast_rows(rows, steps):
    r = next(r for r in range(BF16_TILE_ROWS, rows + 1, BF16_TILE_ROWS)
             if rows % r == 0 and rows // r <= steps)
    return r, rows // r


def _cast_plumbing(casts, steps):
    in_specs, out_specs, out_shapes = [], [], []
    for w, layer in casts:
        _, rows, cols = w.shape
        r, nblk = _cast_rows(rows, steps)
        in_specs.append(pl.BlockSpec((None, r, cols), functools.partial(
            lambda i, layer, nblk: (layer, jnp.minimum(i, nblk - 1), 0), layer=layer, nblk=nblk)))
        out_specs.append(pl.BlockSpec((r, cols), functools.partial(
            lambda i, nblk: (jnp.minimum(i, nblk - 1), 0), nblk=nblk)))
        out_shapes.append(jax.ShapeDtypeStruct((rows, cols), BF16))
    return in_specs, out_specs, out_shapes


def _cast_blocks(src_refs, dst_refs):
    for src, dst in zip(src_refs, dst_refs):
        dst[...] = src[...].astype(BF16)


def _rms_scale(x):
    return lax.rsqrt(jnp.mean(x * x, axis=-1, keepdims=True) + EPS)


def _ffn_hidden(h, win_ref, act_ref, fillers=()):
    d_ff = act_ref.shape[-1]
    chunks = range(0, d_ff, FF_CHUNK)
    fillers = list(fillers)
    per_chunk = -(-len(fillers) // len(chunks))
    for c in chunks:
        e = min(c + FF_CHUNK, d_ff)
        gate = jnp.dot(h, win_ref[:, c:e], preferred_element_type=F32)
        up = jnp.dot(h, win_ref[:, d_ff + c:d_ff + e], preferred_element_type=F32)
        act_ref[:, c:e] = (gate * jax.nn.sigmoid(gate) * up).astype(BF16)
        for fill in fillers[:per_chunk]:
            fill()
        del fillers[:per_chunk]


def _ffn_down(act_ref, wout_ref):
    return jnp.dot(act_ref[...], wout_ref[...], preferred_element_type=F32)


def _head_mean_sq(y):
    sq = y * y
    lo_half = lax.broadcasted_iota(jnp.int32, sq.shape, 1) < HEAD_DIM
    s_lo = jnp.sum(jnp.where(lo_half, sq, 0.0), axis=-1, keepdims=True)
    s_hi = jnp.sum(jnp.where(lo_half, 0.0, sq), axis=-1, keepdims=True)
    return jnp.where(lo_half, s_lo, s_hi) * (1.0 / HEAD_DIM)


def _qk_normed(y, gain):
    return y * lax.rsqrt(_head_mean_sq(y) + EPS) * gain


def _qkv_work(x_ref, g_ref, w_ref, gq_ref, a_refs, qb_ref, kb_ref, vb_ref, h_ref, scr_ref):
    n_d = len(DILATIONS)
    tm = x_ref.shape[0]
    n_blk = MIX_A // LANE

    def normalise():
        x = x_ref[...]
        h_ref[...] = (x * _rms_scale(x) * g_ref[...]).astype(BF16)

    def project_a(gi, normed):
        col = gi * MIX_A
        y = jnp.dot(h_ref[...], w_ref[:, col:col + MIX_A], preferred_element_type=F32)
        for c in range(n_blk):
            yc = y[:, c * LANE:(c + 1) * LANE]
            if normed:
                yc = _qk_normed(yc, gq_ref[:, col + c * LANE:col + (c + 1) * LANE])
            scr_ref[gi * n_blk + c] = yc

    def store_a(gi, di, d):
        out = a_refs[gi * n_d + di]
        for r in range(d):
            for c in range(n_blk):
                slab = scr_ref.at[gi * n_blk + c]
                rows = slab[...] if d == 1 else slab[pl.ds(r, tm // d, stride=d), :]
                out[:, r * MIX_A + c * LANE:r * MIX_A + (c + 1) * LANE] = rows.astype(BF16)

    def project_qb():
        col = 3 * MIX_A
        yq = jnp.dot(h_ref[...], w_ref[:, col:col + MIX_A], preferred_element_type=F32)
        for c in range(n_blk):
            cols = slice(c * LANE, (c + 1) * LANE)
            gain = gq_ref[:, col + c * LANE:col + (c + 1) * LANE]
            qb_ref[:, cols] = _qk_normed(yq[:, cols], gain).astype(BF16)

    def project_kvb():
        col = 4 * MIX_A
        ykv = jnp.dot(h_ref[...], w_ref[:, col:col + 2 * KV_B], preferred_element_type=F32)
        lane = lax.broadcasted_iota(jnp.int32, (tm, LANE), 1)
        for ref, t in ((kb_ref, _qk_normed(ykv[:, :KV_B], gq_ref[:, col:col + KV_B])),
                       (vb_ref, ykv[:, KV_B:])):
            swapped = pltpu.roll(t, HEAD_DIM, 1)
            ref[:, :LANE] = jnp.where(lane < HEAD_DIM, t, swapped).astype(BF16)
            ref[:, LANE:] = jnp.where(lane < HEAD_DIM, swapped, t).astype(BF16)

    work = [normalise]
    for gi, normed in enumerate((True, True, False)):
        work.append(functools.partial(project_a, gi, normed))
        work += [functools.partial(store_a, gi, di, d) for di, d in enumerate(DILATIONS)]
    return work + [project_qb, project_kvb]


def _ffn_qkv_kernel(x_ref, g1_ref, win_ref, wout_ref, gm_ref, wqkv_ref, gq_ref, *refs, n_cast):
    n_d = len(DILATIONS)
    cast_src, refs = refs[:n_cast], refs[n_cast:]
    x1_ref, a_refs, refs = refs[0], refs[1:1 + 3 * n_d], refs[1 + 3 * n_d:]
    (qb_ref, kb_ref, vb_ref), refs = refs[:3], refs[3:]
    cast_dst, (x1_carry, x1_work, act_ref, hm_ref, scr_ref) = refs[:n_cast], refs[n_cast:]
    i = pl.program_id(0)
    last = pl.num_programs(0) - 1
    _cast_blocks(cast_src, cast_dst)

    def qkv_work(src_ref):
        return _qkv_work(src_ref, gm_ref, wqkv_ref, gq_ref, a_refs, qb_ref, kb_ref, vb_ref,
                         hm_ref, scr_ref)

    def stage1(fillers=()):
        x = x_ref[...]
        _ffn_hidden((x * _rms_scale(x) * g1_ref[...]).astype(BF16), win_ref, act_ref, fillers)
        x1 = x_ref[...] + 0.5 * _ffn_down(act_ref, wout_ref)
        x1_ref[...] = x1
        x1_carry[...] = x1

    @pl.when(i == 0)
    def _():
        stage1()

    @pl.when(jnp.logical_and(i > 0, i < last))
    def _():
        x1_work[...] = x1_carry[...]
        stage1(qkv_work(x1_work))

    @pl.when(i == last)
    def _():
        for work in qkv_work(x1_carry):
            work()


def _ffn_qkv(x, g1, w_in, w_out, gm, w_qkv, gains, layer, casts, *, tm=ROW_TILE):
    T, D = x.shape
    nt = T // tm
    d_ff = w_out.shape[0]
    cast_in, cast_out, cast_shapes = _cast_plumbing(casts, nt + 1)
    cur = lambda i: (jnp.minimum(i, nt - 1), 0)
    prev = lambda i: (jnp.maximum(i - 1, 0), 0)
    gain = lambda width: _resident((None, 1, width), lambda i: (layer, 0, 0))
    a_shapes = [(d, MIX_A) for _ in range(3) for d in DILATIONS]
    b_widths = [MIX_A, N_KV_B * LANE, N_KV_B * LANE]
    shapes = [(T // d, d * c) for d, c in a_shapes] + [(T, c) for c in b_widths]
    blocks = [(tm // d, d * c) for d, c in a_shapes] + [(tm, c) for c in b_widths]
    outs = pl.pallas_call(
        functools.partial(_ffn_qkv_kernel, n_cast=len(casts)),
        grid=(nt + 1,),
        in_specs=[
            pl.BlockSpec((tm, D), cur), gain(D), _whole(w_in), _whole(w_out), gain(D),
            _whole(w_qkv), gain(gains.shape[-1]),
        ] + cast_in,
        out_specs=([pl.BlockSpec((tm, D), cur)] + [pl.BlockSpec(b, prev) for b in blocks]
                   + cast_out),
        out_shape=([jax.ShapeDtypeStruct((T, D), F32)]
                   + [jax.ShapeDtypeStruct(s, BF16) for s in shapes] + cast_shapes),
        scratch_shapes=[pltpu.VMEM((tm, D), F32), pltpu.VMEM((tm, D), F32),
                        pltpu.VMEM((tm, d_ff), BF16), pltpu.VMEM((tm, D), BF16),
                        pltpu.VMEM((3 * MIX_A // LANE, tm, LANE), F32)],
        compiler_params=_cparams(("arbitrary",)),
        name="ffn_qkv",
    )(x, g1, w_in, w_out, gm, w_qkv, gains, *[w for w, _ in casts])
    n_d = len(DILATIONS)
    x1, outs, cast = outs[0], outs[1:len(outs) - len(casts)], outs[len(outs) - len(casts):]
    return x1, (outs[:n_d], outs[n_d:2 * n_d], outs[2 * n_d:3 * n_d]), outs[3 * n_d:], cast


def _t5_bucket_np(rel):
    half = N_BUCKETS // 2
    max_exact = half // 2
    ret = np.where(rel > 0, half, 0)
    n = np.abs(rel)
    nf = np.maximum(n, 1).astype(np.float64)
    large = max_exact + (np.log(nf / max_exact) / math.log(MAX_DISTANCE / max_exact)
                         * (half - max_exact)).astype(np.int64)
    large = np.minimum(large, half - 1)
    return ret + np.where(n < max_exact, n, large)


def _bucket_tiles(radius, dilation):
    w = QBLOCK + 2 * radius
    key = np.arange(w)[:, None]
    q = np.arange(QBLOCK)[None, :]
    tiles = []
    for off in (0, -radius, -2 * radius):
        rel = off + key - q
        tiles.append(np.where(np.abs(rel) <= radius, _t5_bucket_np(rel * dilation), -1))
    return np.stack(tiles).astype(np.int32)


def _bias_kernel(rb_ref, idx_ref, o_ref, *, head0, buckets):
    idx = idx_ref[...]
    for h in range(N_HEADS):
        tile = jnp.full(idx.shape, NEG, F32)
        for b in buckets:
            tile = jnp.where(idx == b, rb_ref[b, head0 + h] * LOG2E, tile)
        o_ref[h] = tile


def _bias_tiles(rel_bias, radius, dilation, head0):
    idx_np = _bucket_tiles(radius, dilation)
    buckets = tuple(int(b) for b in np.unique(idx_np[idx_np >= 0]))
    nv, w, bq = idx_np.shape
    return pl.pallas_call(
        functools.partial(_bias_kernel, head0=head0, buckets=buckets),
        grid=(nv,),
        in_specs=[
            pl.BlockSpec(memory_space=pltpu.SMEM),
            pl.BlockSpec((None, w, bq), lambda v: (v, 0, 0)),
        ],
        out_specs=pl.BlockSpec((None, N_HEADS, w, bq), lambda v: (v, 0, 0, 0)),
        out_shape=jax.ShapeDtypeStruct((nv, N_HEADS, w, bq), F32),
        compiler_params=_cparams(("arbitrary",)),
        name="bias_tiles",
    )(rel_bias, jnp.asarray(idx_np))


def _window(i, nb, radius, length):
    w = QBLOCK + 2 * radius
    start = pl.multiple_of(jnp.clip(i * QBLOCK - radius, 0, length - w), radius)
    var = jnp.where(i == 0, 0, jnp.where(i == nb - 1, 2, 1))
    return start, var


_NT = (((1,), (1,)), ((), ()))
_TN = (((0,), (0,)), ((), ()))


def _softmax_rows(s, sink=None):
    m = jnp.max(s, axis=0, keepdims=True)
    if sink is not None:
        m = jnp.maximum(m, sink)
    p = jnp.exp2(s - m)
    den = jnp.sum(p, axis=0, keepdims=True)
    if sink is not None:
        den = den + jnp.exp2(sink - m)
    return p.astype(BF16), 1.0 / den, m + jnp.log2(den)


def _pair_attention(qp, kp, vp, bias_lo, bias_hi, sink_lo=None, sink_hi=None):
    lane = lax.broadcasted_iota(jnp.int32, qp.shape, 1)
    zero = jnp.zeros_like(qp)
    vt = vp.T
    halves = []
    for keep, bias, sink in ((lane < HEAD_DIM, bias_lo, sink_lo), (lane >= HEAD_DIM, bias_hi, sink_hi)):
        s = lax.dot_general(kp, jnp.where(keep, qp, zero), _NT, preferred_element_type=F32)
        p, inv, lse = _softmax_rows(s + bias, sink)
        halves.append((jnp.dot(vt, p, preferred_element_type=F32) * inv, lse))
    (o_lo, lse_lo), (o_hi, lse_hi) = halves
    o_t = jnp.concatenate([o_lo[:HEAD_DIM], o_hi[HEAD_DIM:]], axis=0)
    return o_t.astype(BF16).T, lse_lo, lse_hi


def _attn_a_kernel(q_ref, k_ref, v_ref, bias_ref, o_ref, lse_ref, *,
                   radius, length, nb, group, classes):
    w = QBLOCK + 2 * radius
    for cl, g in [(cl, g) for cl in range(classes) for g in range(group)]:
        start, var = _window(pl.program_id(2) * group + g, nb, radius, length)
        rows = slice(g * QBLOCK, (g + 1) * QBLOCK)
        lses = []
        for pr in range(N_HEADS // 2):
            cols = slice(cl * MIX_A + pr * LANE, cl * MIX_A + (pr + 1) * LANE)
            o, lse_lo, lse_hi = _pair_attention(
                q_ref[rows, cols], k_ref[pl.ds(start, w), cols], v_ref[pl.ds(start, w), cols],
                bias_ref[var, 2 * pr], bias_ref[var, 2 * pr + 1])
            o_ref[rows, cols] = o
            lses += [lse_lo, lse_hi]
        lse_t = jnp.concatenate([jnp.broadcast_to(l, (LSE_LANES, QBLOCK)) for l in lses], axis=0)
        lse_ref[rows, cl * LANE:(cl + 1) * LANE] = lse_t.T


def _attn_a(q, k, v, bias, batch, *, window, dilation):
    d = dilation
    L = q.shape[0] // batch
    nb = L // QBLOCK
    radius = window // (2 * d)
    group = min(nb, ATTN_GROUP)
    classes = min(d, ATTN_GROUP // group)
    seq = lambda t: t.reshape(batch, L, t.shape[-1])
    o, lse = pl.pallas_call(
        functools.partial(_attn_a_kernel, radius=radius, length=L, nb=nb, group=group,
                          classes=classes),
        grid=(batch, d // classes, nb // group),
        in_specs=[
            pl.BlockSpec((None, group * QBLOCK, classes * MIX_A), lambda b, r, i: (b, i, r)),
            pl.BlockSpec((None, L, classes * MIX_A), lambda b, r, i: (b, 0, r)),
            pl.BlockSpec((None, L, classes * MIX_A), lambda b, r, i: (b, 0, r)),
            _resident(bias.shape, lambda b, r, i: (0, 0, 0, 0)),
        ],
        out_specs=[
            pl.BlockSpec((None, group * QBLOCK, classes * MIX_A), lambda b, r, i: (b, i, r)),
            pl.BlockSpec((None, group * QBLOCK, classes * LANE), lambda b, r, i: (b, i, r)),
        ],
        out_shape=[
            jax.ShapeDtypeStruct((batch, L, d * MIX_A), BF16),
            jax.ShapeDtypeStruct((batch, L, d * LANE), F32),
        ],
        compiler_params=_cparams(("parallel", "parallel", "arbitrary")),
        name=f"attn_a_d{d}",
    )(seq(q), seq(k), seq(v), bias)
    return o.reshape(batch * L, d * MIX_A), lse.reshape(batch * L, d * LANE)


def _attn_b_kernel(sink_ref, q_ref, k_ref, v_ref, bias_ref, o_ref, *, radius, length, nb, group):
    w = QBLOCK + 2 * radius
    for g in range(group):
        start, var = _window(pl.program_id(1) * group + g, nb, radius, length)
        rows = slice(g * QBLOCK, (g + 1) * QBLOCK)
        for pr in range(N_HEADS // 2):
            cols = slice(pr * LANE, (pr + 1) * LANE)
            kv_cols = slice(2 * pr // GROUP_B * LANE, (2 * pr // GROUP_B + 1) * LANE)
            lo, hi = 2 * pr, 2 * pr + 1
            o, _, _ = _pair_attention(
                q_ref[rows, cols], k_ref[pl.ds(start, w), kv_cols], v_ref[pl.ds(start, w), kv_cols],
                bias_ref[var, lo], bias_ref[var, hi], sink_ref[lo] * LOG2E, sink_ref[hi] * LOG2E)
            o_ref[rows, cols] = o


def _attn_b(q, k, v, bias, sink, batch):
    S = q.shape[0] // batch
    nb = S // QBLOCK
    group = min(nb, ATTN_GROUP)
    seq = lambda t: t.reshape(batch, S, t.shape[-1])
    o = pl.pallas_call(
        functools.partial(_attn_b_kernel, radius=SWA_RADIUS, length=S, nb=nb, group=group),
        grid=(batch, nb // group),
        in_specs=[
            pl.BlockSpec(memory_space=pltpu.SMEM),
            pl.BlockSpec((None, group * QBLOCK, MIX_A), lambda b, i: (b, i, 0)),
            pl.BlockSpec((None, S, N_KV_B * LANE), lambda b, i: (b, 0, 0)),
            pl.BlockSpec((None, S, N_KV_B * LANE), lambda b, i: (b, 0, 0)),
            _resident(bias.shape, lambda b, i: (0, 0, 0, 0)),
        ],
        out_specs=pl.BlockSpec((None, group * QBLOCK, MIX_A), lambda b, i: (b, i, 0)),
        out_shape=jax.ShapeDtypeStruct((batch, S, MIX_A), BF16),
        compiler_params=_cparams(("parallel", "arbitrary")),
        name="attn_b",
    )(sink, seq(q), seq(k), seq(v), bias)
    return o.reshape(batch * S, MIX_A)


def _merge_work(o_refs, l_refs, oscr_ref, lscr_ref, mrg_ref):
    n_d = len(DILATIONS)
    tm = mrg_ref.shape[0]
    n_blk = MIX_A // LANE

    def unpermute(di, d, r):
        rows = slice(None) if d == 1 else pl.ds(r, tm // d, stride=d)
        lscr_ref[di, rows, :] = l_refs[di][:, r * LANE:(r + 1) * LANE]
        for c in range(n_blk):
            src = o_refs[di][:, r * MIX_A + c * LANE:r * MIX_A + (c + 1) * LANE]
            oscr_ref[di * n_blk + c, rows, :] = src.astype(F32)

    def weights():
        lse = [lscr_ref[di] for di in range(n_d)]
        mx = functools.reduce(jnp.maximum, lse)
        ex = [jnp.exp2(l - mx) for l in lse]
        inv = 1.0 / functools.reduce(jnp.add, ex)
        for di, e in enumerate(ex):
            lscr_ref[di] = e * inv

    def merge(c):
        lane = lax.broadcasted_iota(jnp.int32, (tm, LANE), 1)
        lo, hi = 2 * c * LSE_LANES, (2 * c + 1) * LSE_LANES
        merged = jnp.zeros((tm, LANE), F32)
        for di in range(n_d):
            wt = lscr_ref[di]
            wfull = jnp.where(lane < HEAD_DIM, wt[:, lo:lo + 1], wt[:, hi:hi + 1])
            merged = merged + wfull * oscr_ref[di * n_blk + c]
        mrg_ref[:, c * LANE:(c + 1) * LANE] = merged.astype(BF16)

    work = [functools.partial(unpermute, di, d, r)
            for di, d in enumerate(DILATIONS) for r in range(d)]
    return work + [weights] + [functools.partial(merge, c) for c in range(n_blk)]


def _mix_ffn_ple_kernel(x_ref, *refs, n_cast):
    n_d = len(DILATIONS)
    o_refs, l_refs, refs = refs[:n_d], refs[n_d:2 * n_d], refs[2 * n_d:]
    (ob_ref, p_ref, wo_ref, g2_ref, win_ref, wout_ref, gp_ref, wg_ref, wp_ref), refs = (
        refs[:9], refs[9:])
    cast_src, out_ref, cast_dst, refs = (refs[:n_cast], refs[n_cast],
                                         refs[n_cast + 1:2 * n_cast + 1], refs[2 * n_cast + 1:])
    x2_carry, h_carry, x2_work, h_work, act_ref, oscr_ref, lscr_ref, mrg_ref = refs

    i = pl.program_id(0)
    last = pl.num_programs(0) - 1
    _cast_blocks(cast_src, cast_dst)

    def merge_work():
        return _merge_work(o_refs, l_refs, oscr_ref, lscr_ref, mrg_ref)

    def project():
        x2 = (x_ref[...]
              + jnp.dot(mrg_ref[...], wo_ref[:MIX_A, :], preferred_element_type=F32)
              + jnp.dot(ob_ref[...], wo_ref[MIX_A:, :], preferred_element_type=F32))
        x2_carry[...] = x2
        h_carry[...] = (x2 * _rms_scale(x2) * g2_ref[...]).astype(BF16)

    def stage2(x2_ref, h_ref, fillers=(), between=lambda: None):
        _ffn_hidden(h_ref[...], win_ref, act_ref, fillers)
        x3 = x2_ref[...] + 0.5 * _ffn_down(act_ref, wout_ref)
        hp = (x3 * _rms_scale(x3) * gp_ref[...]).astype(BF16)
        between()
        gate = jax.nn.sigmoid(jnp.dot(hp, wg_ref[...], preferred_element_type=F32))
        proj = jnp.dot(p_ref[...].astype(BF16), wp_ref[...], preferred_element_type=F32)
        out_ref[...] = x3 + gate * proj

    @pl.when(i == 0)
    def _():
        for work in merge_work():
            work()
        project()

    @pl.when(jnp.logical_and(i > 0, i < last))
    def _():
        x2_work[...] = x2_carry[...]
        h_work[...] = h_carry[...]
        stage2(x2_work, h_work, merge_work(), project)

    @pl.when(i == last)
    def _():
        stage2(x2_carry, h_carry)


def _mix_ffn_ple(x, outs, lses, ob, p, w_o, g2, w_in, w_out, gp, w_gate, w_proj, layer, casts,
                 *, tm=ROW_TILE):
    T, D = x.shape
    nt = T // tm
    d_ff = w_out.shape[0]
    n_d = len(DILATIONS)
    cast_in, cast_out, cast_shapes = _cast_plumbing(casts, nt + 1)
    cur = lambda i: (jnp.minimum(i, nt - 1), 0)
    prev = lambda i: (jnp.maximum(i - 1, 0), 0)
    gain = lambda: _resident((None, 1, D), lambda i: (layer, 0, 0))
    res = pl.pallas_call(
        functools.partial(_mix_ffn_ple_kernel, n_cast=len(casts)),
        grid=(nt + 1,),
        in_specs=([pl.BlockSpec((tm, D), cur)]
                  + [pl.BlockSpec((tm // d, d * MIX_A), cur) for d in DILATIONS]
                  + [pl.BlockSpec((tm // d, d * LANE), cur) for d in DILATIONS]
                  + [pl.BlockSpec((tm, MIX_A), cur),
                     pl.BlockSpec((None, tm, p.shape[-1]),
                                  lambda i: (layer, jnp.maximum(i - 1, 0), 0)),
                     _whole(w_o), gain(), _whole(w_in), _whole(w_out), gain(), _whole(w_gate),
                     _layer_block(w_proj, layer)]
                  + cast_in),
        out_specs=[pl.BlockSpec((tm, D), prev)] + cast_out,
        out_shape=[jax.ShapeDtypeStruct((T, D), F32)] + cast_shapes,
        scratch_shapes=[pltpu.VMEM((tm, D), F32), pltpu.VMEM((tm, D), BF16),
                        pltpu.VMEM((tm, D), F32), pltpu.VMEM((tm, D), BF16),
                        pltpu.VMEM((tm, d_ff), BF16),
                        pltpu.VMEM((n_d * MIX_A // LANE, tm, LANE), F32),
                        pltpu.VMEM((n_d, tm, LANE), F32),
                        pltpu.VMEM((tm, MIX_A), BF16)],
        compiler_params=_cparams(("arbitrary",)),
        name="mix_ffn_ple",
    )(x, *outs, *lses, ob, p, w_o, g2, w_in, w_out, gp, w_gate, w_proj, *[w for w, _ in casts])
    return res[0], res[1:]


def kernel(x, p, rel_bias, norm_ffn1, ffn1_w_in, ffn1_w_out, norm_mix, w_qkv, q_norm_a, k_norm_a, q_norm_b, k_norm_b, sink_b, w_o, norm_ffn2, ffn2_w_in, ffn2_w_out, norm_ple, w_ple_gate, w_ple_proj):
    B, S, D = x.shape
    depth = p.shape[0]
    T = B * S
    x = x.reshape(T, D)
    p = p.reshape(depth, T, p.shape[-1])

    bias_a = [_bias_tiles(rel_bias, window // (2 * d), d, 0) for window, d in DILATED_CONFIGS]
    bias_b = _bias_tiles(rel_bias, SWA_RADIUS, 1, N_HEADS)

    scale = HEAD_DIM ** -0.5 * LOG2E
    tile = lambda g, n: jnp.tile(g, (1, n))
    gains = jnp.concatenate([
        tile(q_norm_a * scale, N_HEADS), tile(k_norm_a, N_HEADS),
        jnp.ones((depth, MIX_A), F32),
        tile(q_norm_b * scale, N_HEADS), tile(k_norm_b, N_KV_B),
        jnp.ones((depth, KV_B), F32)], axis=1)[:, None]
    row = lambda g: g[:, None]
    first_half = (ffn1_w_in, ffn1_w_out, w_qkv)
    second_half = (w_o, ffn2_w_in, ffn2_w_out, w_ple_gate)
    w_first = [w[0].astype(BF16) for w in first_half]
    w_proj_b = w_ple_proj.astype(BF16)

    for i in range(depth):
        x, (qa, ka, va), (qb, kb, vb), w_second = _ffn_qkv(
            x, row(norm_ffn1), w_first[0], w_first[1], row(norm_mix), w_first[2], gains, i,
            [(w, i) for w in second_half])
        outs, lses = [], []
        for di, (window, d) in enumerate(DILATED_CONFIGS):
            o, lse = _attn_a(qa[di], ka[di], va[di], bias_a[di], B, window=window, dilation=d)
            outs.append(o)
            lses.append(lse)
        ob = _attn_b(qb, kb, vb, bias_b, sink_b[i], B)
        w_o_b, w_in2, w_out2, w_gate_b = w_second
        x, w_first = _mix_ffn_ple(
            x, outs, lses, ob, p, w_o_b, row(norm_ffn2), w_in2, w_out2, row(norm_ple), w_gate_b,
            w_proj_b, i, [(w, i + 1) for w in first_half] if i + 1 < depth else [])
    return x.reshape(B, S, D)
```

```python
import functools
import math

import jax
import jax.numpy as jnp
import numpy as np
from jax import lax
from jax.experimental import pallas as pl
from jax.experimental.pallas import tpu as pltpu

HEAD_DIM = 64
N_HEADS = 8
N_KV_B = 2
GROUP_B = N_HEADS // N_KV_B
MIX_A = N_HEADS * HEAD_DIM
KV_B = N_KV_B * HEAD_DIM
DILATED_CONFIGS = ((128, 1), (512, 4), (2048, 16))
DILATIONS = tuple(d for _, d in DILATED_CONFIGS)
SWA_RADIUS = 128
N_BUCKETS = 32
MAX_DISTANCE = 1024
EPS = 1e-6
NEG = -1e30
LOG2E = math.log2(math.e)
QBLOCK = 128
ATTN_GROUP = 8
LSE_LANES = 16
LANE = 128
BF16_TILE_ROWS = 16
FF_CHUNK = 256
ROW_TILE = 512

VMEM_LIMIT = 58 * 1024 * 1024

BF16 = jnp.bfloat16
F32 = jnp.float32


def _cparams(sem):
    return pltpu.CompilerParams(dimension_semantics=sem, vmem_limit_bytes=VMEM_LIMIT)


def _resident(shape, index_map):
    return pl.BlockSpec(shape, index_map, pipeline_mode=pl.Buffered(1))


def _layer_block(w, layer):
    return _resident((None,) + w.shape[1:], lambda i: (layer, 0, 0))


def _whole(w):
    return _resident(w.shape, lambda i: (0, 0))


def _cast_rows(rows, steps):
    r = next(r for r in range(BF16_TILE_ROWS, rows + 1, BF16_TILE_ROWS)
             if rows % r == 0 and rows // r <= steps)
    return r, rows // r


def _cast_plumbing(casts, steps):
    in_specs, out_specs, out_shapes = [], [], []
    for w, layer in casts:
        _, rows, cols = w.shape
        r, nblk = _cast_rows(rows, steps)
        in_specs.append(pl.BlockSpec((None, r, cols), functools.partial(
            lambda i, layer, nblk: (layer, jnp.minimum(i, nblk - 1), 0), layer=layer, nblk=nblk)))
        out_specs.append(pl.BlockSpec((r, cols), functools.partial(
            lambda i, nblk: (jnp.minimum(i, nblk - 1), 0), nblk=nblk)))
        out_shapes.append(jax.ShapeDtypeStruct((rows, cols), BF16))
    return in_specs, out_specs, out_shapes


def _cast_work(src_refs, dst_refs):
    def cast(src, dst):
        dst[...] = src[...].astype(BF16)
    return [functools.partial(cast, src, dst) for src, dst in zip(src_refs, dst_refs)]


def _rms_scale(x):
    return lax.rsqrt(jnp.mean(x * x, axis=-1, keepdims=True) + EPS)


def _ffn_hidden(h_ref, win_ref, act_ref, fillers=()):
    d_ff = act_ref.shape[-1]
    chunks = range(0, d_ff, FF_CHUNK)
    fillers = list(fillers)
    per_chunk = -(-len(fillers) // len(chunks))
    for c in chunks:
        e = min(c + FF_CHUNK, d_ff)
        h = h_ref[...]
        gate = jnp.dot(h, win_ref[:, c:e], preferred_element_type=F32)
        up = jnp.dot(h, win_ref[:, d_ff + c:d_ff + e], preferred_element_type=F32)
        act_ref[:, c:e] = (gate * jax.nn.sigmoid(gate) * up).astype(BF16)
        for fill in fillers[:per_chunk]:
            fill()
        del fillers[:per_chunk]


def _ffn_down(act_ref, wout_ref):
    return jnp.dot(act_ref[...], wout_ref[...], preferred_element_type=F32)


def _head_mean_sq(y):
    sq = y * y
    lo_half = lax.broadcasted_iota(jnp.int32, sq.shape, 1) < HEAD_DIM
    s_lo = jnp.sum(jnp.where(lo_half, sq, 0.0), axis=-1, keepdims=True)
    s_hi = jnp.sum(jnp.where(lo_half, 0.0, sq), axis=-1, keepdims=True)
    return jnp.where(lo_half, s_lo, s_hi) * (1.0 / HEAD_DIM)


def _qk_normed(y, gain):
    return y * lax.rsqrt(_head_mean_sq(y) + EPS) * gain


def _qkv_work(x_ref, g_ref, w_ref, gq_ref, a_refs, qb_ref, kb_ref, vb_ref, h_ref, scr_ref):
    n_d = len(DILATIONS)
    tm = x_ref.shape[0]
    n_blk = MIX_A // LANE

    def normalise():
        x = x_ref[...]
        h_ref[...] = (x * _rms_scale(x) * g_ref[...]).astype(BF16)

    def project_a(gi, normed):
        col = gi * MIX_A
        y = jnp.dot(h_ref[...], w_ref[:, col:col + MIX_A], preferred_element_type=F32)
        for c in range(n_blk):
            yc = y[:, c * LANE:(c + 1) * LANE]
            if normed:
                yc = _qk_normed(yc, gq_ref[:, col + c * LANE:col + (c + 1) * LANE])
            scr_ref[gi * n_blk + c] = yc

    def store_a(gi, di, d):
        out = a_refs[gi * n_d + di]
        for r in range(d):
            for c in range(n_blk):
                slab = scr_ref.at[gi * n_blk + c]
                rows = slab[...] if d == 1 else slab[pl.ds(r, tm // d, stride=d), :]
                out[:, r * MIX_A + c * LANE:r * MIX_A + (c + 1) * LANE] = rows.astype(BF16)

    def project_qb():
        col = 3 * MIX_A
        yq = jnp.dot(h_ref[...], w_ref[:, col:col + MIX_A], preferred_element_type=F32)
        for c in range(n_blk):
            cols = slice(c * LANE, (c + 1) * LANE)
            gain = gq_ref[:, col + c * LANE:col + (c + 1) * LANE]
            qb_ref[:, cols] = _qk_normed(yq[:, cols], gain).astype(BF16)

    def project_kvb():
        col = 4 * MIX_A
        ykv = jnp.dot(h_ref[...], w_ref[:, col:col + 2 * KV_B], preferred_element_type=F32)
        lane = lax.broadcasted_iota(jnp.int32, (tm, LANE), 1)
        for ref, t in ((kb_ref, _qk_normed(ykv[:, :KV_B], gq_ref[:, col:col + KV_B])),
                       (vb_ref, ykv[:, KV_B:])):
            swapped = pltpu.roll(t, HEAD_DIM, 1)
            ref[:, :LANE] = jnp.where(lane < HEAD_DIM, t, swapped).astype(BF16)
            ref[:, LANE:] = jnp.where(lane < HEAD_DIM, swapped, t).astype(BF16)

    def chain(*thunks):
        def run():
            for t in thunks:
                t()
        return run

    work = []
    for gi, normed in enumerate((True, True, False)):
        project = functools.partial(project_a, gi, normed)
        stores = [functools.partial(store_a, gi, di, d) for di, d in enumerate(DILATIONS)]
        work += [chain(normalise, project) if gi == 0 else project,
                 chain(*stores[:-1]), stores[-1]]
    return work + [project_qb, project_kvb]


def _ffn_qkv_kernel(x_ref, g1_ref, win_ref, wout_ref, gm_ref, wqkv_ref, gq_ref, *refs, n_cast):
    n_d = len(DILATIONS)
    cast_src, refs = refs[:n_cast], refs[n_cast:]
    x1_ref, a_refs, refs = refs[0], refs[1:1 + 3 * n_d], refs[1 + 3 * n_d:]
    (qb_ref, kb_ref, vb_ref), refs = refs[:3], refs[3:]
    cast_dst, (x1_carry, x1_work, act_ref, h1_ref, hm_ref, scr_ref) = refs[:n_cast], refs[n_cast:]
    i = pl.program_id(0)
    last = pl.num_programs(0) - 1
    cast_work = _cast_work(cast_src, cast_dst)

    def qkv_work(src_ref):
        return _qkv_work(src_ref, gm_ref, wqkv_ref, gq_ref, a_refs, qb_ref, kb_ref, vb_ref,
                         hm_ref, scr_ref)

    def stage1(fillers=()):
        x = x_ref[...]
        h1_ref[...] = (x * _rms_scale(x) * g1_ref[...]).astype(BF16)
        _ffn_hidden(h1_ref, win_ref, act_ref, fillers)
        x1 = x_ref[...] + 0.5 * _ffn_down(act_ref, wout_ref)
        x1_ref[...] = x1
        x1_carry[...] = x1

    @pl.when(i == 0)
    def _():
        stage1(cast_work)

    @pl.when(jnp.logical_and(i > 0, i < last))
    def _():
        x1_work[...] = x1_carry[...]
        stage1(qkv_work(x1_work) + cast_work)

    @pl.when(i == last)
    def _():
        for work in qkv_work(x1_carry) + cast_work:
            work()


def _ffn_qkv(x, g1, w_in, w_out, gm, w_qkv, gains, layer, casts, *, tm=ROW_TILE):
    T, D = x.shape
    nt = T // tm
    d_ff = w_out.shape[0]
    cast_in, cast_out, cast_shapes = _cast_plumbing(casts, nt + 1)
    cur = lambda i: (jnp.minimum(i, nt - 1), 0)
    prev = lambda i: (jnp.maximum(i - 1, 0), 0)
    gain = lambda width: _resident((None, 1, width), lambda i: (layer, 0, 0))
    a_shapes = [(d, MIX_A) for _ in range(3) for d in DILATIONS]
    b_widths = [MIX_A, N_KV_B * LANE, N_KV_B * LANE]
    shapes = [(T // d, d * c) for d, c in a_shapes] + [(T, c) for c in b_widths]
    blocks = [(tm // d, d * c) for d, c in a_shapes] + [(tm, c) for c in b_widths]
    outs = pl.pallas_call(
        functools.partial(_ffn_qkv_kernel, n_cast=len(casts)),
        grid=(nt + 1,),
        in_specs=[
            pl.BlockSpec((tm, D), cur), gain(D), _whole(w_in), _whole(w_out), gain(D),
            _whole(w_qkv), gain(gains.shape[-1]),
        ] + cast_in,
        out_specs=([pl.BlockSpec((tm, D), cur)] + [pl.BlockSpec(b, prev) for b in blocks]
                   + cast_out),
        out_shape=([jax.ShapeDtypeStruct((T, D), F32)]
                   + [jax.ShapeDtypeStruct(s, BF16) for s in shapes] + cast_shapes),
        scratch_shapes=[pltpu.VMEM((tm, D), F32), pltpu.VMEM((tm, D), F32),
                        pltpu.VMEM((tm, d_ff), BF16), pltpu.VMEM((tm, D), BF16),
                        pltpu.VMEM((tm, D), BF16),
                        pltpu.VMEM((3 * MIX_A // LANE, tm, LANE), F32)],
        compiler_params=_cparams(("arbitrary",)),
        name="ffn_qkv",
    )(x, g1, w_in, w_out, gm, w_qkv, gains, *[w for w, _ in casts])
    n_d = len(DILATIONS)
    x1, outs, cast = outs[0], outs[1:len(outs) - len(casts)], outs[len(outs) - len(casts):]
    return x1, (outs[:n_d], outs[n_d:2 * n_d], outs[2 * n_d:3 * n_d]), outs[3 * n_d:], cast


def _t5_bucket_np(rel):
    half = N_BUCKETS // 2
    max_exact = half // 2
    ret = np.where(rel > 0, half, 0)
    n = np.abs(rel)
    nf = np.maximum(n, 1).astype(np.float64)
    large = max_exact + (np.log(nf / max_exact) / math.log(MAX_DISTANCE / max_exact)
                         * (half - max_exact)).astype(np.int64)
    large = np.minimum(large, half - 1)
    return ret + np.where(n < max_exact, n, large)


def _bucket_tiles(radius, dilation):
    w = QBLOCK + 2 * radius
    key = np.arange(w)[:, None]
    q = np.arange(QBLOCK)[None, :]
    tiles = []
    for off in (0, -radius, -2 * radius):
        rel = off + key - q
        tiles.append(np.where(np.abs(rel) <= radius, _t5_bucket_np(rel * dilation), -1))
    return np.stack(tiles).astype(np.int32)


def _bias_kernel(rb_ref, idx_ref, o_ref, *, head0, buckets):
    idx = idx_ref[...]
    for h in range(N_HEADS):
        tile = jnp.full(idx.shape, NEG, F32)
        for b in buckets:
            tile = jnp.where(idx == b, rb_ref[b, head0 + h] * LOG2E, tile)
        o_ref[h] = tile


def _bias_tiles(rel_bias, radius, dilation, head0):
    idx_np = _bucket_tiles(radius, dilation)
    buckets = tuple(int(b) for b in np.unique(idx_np[idx_np >= 0]))
    nv, w, bq = idx_np.shape
    return pl.pallas_call(
        functools.partial(_bias_kernel, head0=head0, buckets=buckets),
        grid=(nv,),
        in_specs=[
            pl.BlockSpec(memory_space=pltpu.SMEM),
            pl.BlockSpec((None, w, bq), lambda v: (v, 0, 0)),
        ],
        out_specs=pl.BlockSpec((None, N_HEADS, w, bq), lambda v: (v, 0, 0, 0)),
        out_shape=jax.ShapeDtypeStruct((nv, N_HEADS, w, bq), F32),
        compiler_params=_cparams(("arbitrary",)),
        name="bias_tiles",
    )(rel_bias, jnp.asarray(idx_np))


def _window(i, nb, radius, length):
    w = QBLOCK + 2 * radius
    start = pl.multiple_of(jnp.clip(i * QBLOCK - radius, 0, length - w), radius)
    var = jnp.where(i == 0, 0, jnp.where(i == nb - 1, 2, 1))
    return start, var


_NT = (((1,), (1,)), ((), ()))
_TN = (((0,), (0,)), ((), ()))


def _pair_attention(qp, kp, vp, bias_lo, bias_hi, sink_lo=None, sink_hi=None):
    lane = lax.broadcasted_iota(jnp.int32, qp.shape, 1)
    zero = jnp.zeros_like(qp)
    vt = vp.T
    ones = jnp.ones((HEAD_DIM, vt.shape[1]), vt.dtype)
    outs, lses = [], []
    for hi, (bias, sink) in enumerate(((bias_lo, sink_lo), (bias_hi, sink_hi))):
        keep = (lane >= HEAD_DIM) if hi else (lane < HEAD_DIM)
        s = lax.dot_general(kp, jnp.where(keep, qp, zero), _NT, preferred_element_type=F32) + bias
        m = jnp.max(s, axis=0, keepdims=True)
        if sink is not None:
            m = jnp.maximum(m, sink)
        p = jnp.exp2(s - m).astype(BF16)
        vt_den = jnp.concatenate([ones, vt[HEAD_DIM:]] if hi else [vt[:HEAD_DIM], ones], axis=0)
        o = jnp.dot(vt_den, p, preferred_element_type=F32)
        den = o[:1] if hi else o[HEAD_DIM:HEAD_DIM + 1]
        if sink is not None:
            den = den + jnp.exp2(sink - m)
        outs.append((o[HEAD_DIM:] if hi else o[:HEAD_DIM]) * (1.0 / den))
        lses.append(m + jnp.log2(den))
    o_t = jnp.concatenate(outs, axis=0)
    return o_t.astype(BF16).T, lses[0], lses[1]


def _attn_a_kernel(q_ref, k_ref, v_ref, bias_ref, o_ref, lse_ref, *,
                   radius, length, nb, group, classes):
    w = QBLOCK + 2 * radius
    for cl, g in [(cl, g) for cl in range(classes) for g in range(group)]:
        start, var = _window(pl.program_id(2) * group + g, nb, radius, length)
        rows = slice(g * QBLOCK, (g + 1) * QBLOCK)
        lses = []
        for pr in range(N_HEADS // 2):
            cols = slice(cl * MIX_A + pr * LANE, cl * MIX_A + (pr + 1) * LANE)
            o, lse_lo, lse_hi = _pair_attention(
                q_ref[rows, cols], k_ref[pl.ds(start, w), cols], v_ref[pl.ds(start, w), cols],
                bias_ref[var, 2 * pr], bias_ref[var, 2 * pr + 1])
            o_ref[rows, cols] = o
            lses += [lse_lo, lse_hi]
        lse_t = jnp.concatenate([jnp.broadcast_to(l, (LSE_LANES, QBLOCK)) for l in lses], axis=0)
        lse_ref[rows, cl * LANE:(cl + 1) * LANE] = lse_t.T


def _attn_a(q, k, v, bias, batch, *, window, dilation):
    d = dilation
    L = q.shape[0] // batch
    nb = L // QBLOCK
    radius = window // (2 * d)
    group = min(nb, ATTN_GROUP)
    classes = min(d, ATTN_GROUP // group)
    seq = lambda t: t.reshape(batch, L, t.shape[-1])
    o, lse = pl.pallas_call(
        functools.partial(_attn_a_kernel, radius=radius, length=L, nb=nb, group=group,
                          classes=classes),
        grid=(batch, d // classes, nb // group),
        in_specs=[
            pl.BlockSpec((None, group * QBLOCK, classes * MIX_A), lambda b, r, i: (b, i, r)),
            pl.BlockSpec((None, L, classes * MIX_A), lambda b, r, i: (b, 0, r)),
            pl.BlockSpec((None, L, classes * MIX_A), lambda b, r, i: (b, 0, r)),
            _resident(bias.shape, lambda b, r, i: (0, 0, 0, 0)),
        ],
        out_specs=[
            pl.BlockSpec((None, group * QBLOCK, classes * MIX_A), lambda b, r, i: (b, i, r)),
            pl.BlockSpec((None, group * QBLOCK, classes * LANE), lambda b, r, i: (b, i, r)),
        ],
        out_shape=[
            jax.ShapeDtypeStruct((batch, L, d * MIX_A), BF16),
            jax.ShapeDtypeStruct((batch, L, d * LANE), F32),
        ],
        compiler_params=_cparams(("parallel", "parallel", "arbitrary")),
        name=f"attn_a_d{d}",
    )(seq(q), seq(k), seq(v), bias)
    return o.reshape(batch * L, d * MIX_A), lse.reshape(batch * L, d * LANE)


def _attn_b_kernel(sink_ref, q_ref, k_ref, v_ref, bias_ref, o_ref, *, radius, length, nb, group):
    w = QBLOCK + 2 * radius
    for g in range(group):
        start, var = _window(pl.program_id(1) * group + g, nb, radius, length)
        rows = slice(g * QBLOCK, (g + 1) * QBLOCK)
        for pr in range(N_HEADS // 2):
            cols = slice(pr * LANE, (pr + 1) * LANE)
            kv_cols = slice(2 * pr // GROUP_B * LANE, (2 * pr // GROUP_B + 1) * LANE)
            lo, hi = 2 * pr, 2 * pr + 1
            o, _, _ = _pair_attention(
                q_ref[rows, cols], k_ref[pl.ds(start, w), kv_cols], v_ref[pl.ds(start, w), kv_cols],
                bias_ref[var, lo], bias_ref[var, hi], sink_ref[lo] * LOG2E, sink_ref[hi] * LOG2E)
            o_ref[rows, cols] = o


def _attn_b(q, k, v, bias, sink, batch):
    S = q.shape[0] // batch
    nb = S // QBLOCK
    group = min(nb, ATTN_GROUP)
    seq = lambda t: t.reshape(batch, S, t.shape[-1])
    o = pl.pallas_call(
        functools.partial(_attn_b_kernel, radius=SWA_RADIUS, length=S, nb=nb, group=group),
        grid=(batch, nb // group),
        in_specs=[
            pl.BlockSpec(memory_space=pltpu.SMEM),
            pl.BlockSpec((None, group * QBLOCK, MIX_A), lambda b, i: (b, i, 0)),
            pl.BlockSpec((None, S, N_KV_B * LANE), lambda b, i: (b, 0, 0)),
            pl.BlockSpec((None, S, N_KV_B * LANE), lambda b, i: (b, 0, 0)),
            _resident(bias.shape, lambda b, i: (0, 0, 0, 0)),
        ],
        out_specs=pl.BlockSpec((None, group * QBLOCK, MIX_A), lambda b, i: (b, i, 0)),
        out_shape=jax.ShapeDtypeStruct((batch, S, MIX_A), BF16),
        compiler_params=_cparams(("parallel", "arbitrary")),
        name="attn_b",
    )(sink, seq(q), seq(k), seq(v), bias)
    return o.reshape(batch * S, MIX_A)


def _merge_work(o_refs, l_refs, oscr_ref, lscr_ref, mrg_ref):
    n_d = len(DILATIONS)
    tm = mrg_ref.shape[0]
    n_blk = MIX_A // LANE

    def unpermute(di, d, r):
        rows = slice(None) if d == 1 else pl.ds(r, tm // d, stride=d)
        lscr_ref[di, rows, :] = l_refs[di][:, r * LANE:(r + 1) * LANE]
        for c in range(n_blk):
            src = o_refs[di][:, r * MIX_A + c * LANE:r * MIX_A + (c + 1) * LANE]
            oscr_ref[di * n_blk + c, rows, :] = src.astype(F32)

    def weights():
        lse = [lscr_ref[di] for di in range(n_d)]
        mx = functools.reduce(jnp.maximum, lse)
        ex = [jnp.exp2(l - mx) for l in lse]
        inv = 1.0 / functools.reduce(jnp.add, ex)
        for di, e in enumerate(ex):
            lscr_ref[di] = e * inv

    def merge(c):
        lane = lax.broadcasted_iota(jnp.int32, (tm, LANE), 1)
        lo, hi = 2 * c * LSE_LANES, (2 * c + 1) * LSE_LANES
        merged = jnp.zeros((tm, LANE), F32)
        for di in range(n_d):
            wt = lscr_ref[di]
            wfull = jnp.where(lane < HEAD_DIM, wt[:, lo:lo + 1], wt[:, hi:hi + 1])
            merged = merged + wfull * oscr_ref[di * n_blk + c]
        mrg_ref[:, c * LANE:(c + 1) * LANE] = merged.astype(BF16)

    work = [functools.partial(unpermute, di, d, r)
            for di, d in enumerate(DILATIONS) for r in range(d)]
    return work + [weights] + [functools.partial(merge, c) for c in range(n_blk)]


def _mix_ffn_ple_kernel(x_ref, *refs, n_cast):
    n_d = len(DILATIONS)
    o_refs, l_refs, refs = refs[:n_d], refs[n_d:2 * n_d], refs[2 * n_d:]
    (ob_ref, p_ref, wo_ref, g2_ref, win_ref, wout_ref, gp_ref, wg_ref, wp_ref), refs = (
        refs[:9], refs[9:])
    cast_src, out_ref, cast_dst, refs = (refs[:n_cast], refs[n_cast],
                                         refs[n_cast + 1:2 * n_cast + 1], refs[2 * n_cast + 1:])
    x2_carry, h_carry, x2_work, h_work, act_ref, oscr_ref, lscr_ref, mrg_ref = refs

    i = pl.program_id(0)
    last = pl.num_programs(0) - 1
    cast_work = _cast_work(cast_src, cast_dst)

    def merge_work():
        return _merge_work(o_refs, l_refs, oscr_ref, lscr_ref, mrg_ref)

    def project():
        x2 = (x_ref[...]
              + jnp.dot(mrg_ref[...], wo_ref[:MIX_A, :], preferred_element_type=F32)
              + jnp.dot(ob_ref[...], wo_ref[MIX_A:, :], preferred_element_type=F32))
        x2_carry[...] = x2
        h_carry[...] = (x2 * _rms_scale(x2) * g2_ref[...]).astype(BF16)

    def stage2(x2_ref, h_ref, fillers=(), between=lambda: None):
        _ffn_hidden(h_ref, win_ref, act_ref, fillers)
        x3 = x2_ref[...] + 0.5 * _ffn_down(act_ref, wout_ref)
        hp = (x3 * _rms_scale(x3) * gp_ref[...]).astype(BF16)
        between()
        gate = jax.nn.sigmoid(jnp.dot(hp, wg_ref[...], preferred_element_type=F32))
        proj = jnp.dot(p_ref[...].astype(BF16), wp_ref[...], preferred_element_type=F32)
        out_ref[...] = x3 + gate * proj

    @pl.when(i == 0)
    def _():
        for work in merge_work() + cast_work:
            work()
        project()

    @pl.when(jnp.logical_and(i > 0, i < last))
    def _():
        x2_work[...] = x2_carry[...]
        h_work[...] = h_carry[...]
        stage2(x2_work, h_work, merge_work() + cast_work, project)

    @pl.when(i == last)
    def _():
        stage2(x2_carry, h_carry, cast_work)


def _mix_ffn_ple(x, outs, lses, ob, p, w_o, g2, w_in, w_out, gp, w_gate, w_proj, layer, casts,
                 *, tm=ROW_TILE):
    T, D = x.shape
    nt = T // tm
    d_ff = w_out.shape[0]
    n_d = len(DILATIONS)
    cast_in, cast_out, cast_shapes = _cast_plumbing(casts, nt + 1)
    cur = lambda i: (jnp.minimum(i, nt - 1), 0)
    prev = lambda i: (jnp.maximum(i - 1, 0), 0)
    gain = lambda: _resident((None, 1, D), lambda i: (layer, 0, 0))
    res = pl.pallas_call(
        functools.partial(_mix_ffn_ple_kernel, n_cast=len(casts)),
        grid=(nt + 1,),
        in_specs=([pl.BlockSpec((tm, D), cur)]
                  + [pl.BlockSpec((tm // d, d * MIX_A), cur) for d in DILATIONS]
                  + [pl.BlockSpec((tm // d, d * LANE), cur) for d in DILATIONS]
                  + [pl.BlockSpec((tm, MIX_A), cur),
                     pl.BlockSpec((None, tm, p.shape[-1]),
                                  lambda i: (layer, jnp.maximum(i - 1, 0), 0)),
                     _whole(w_o), gain(), _whole(w_in), _whole(w_out), gain(), _whole(w_gate),
                     _layer_block(w_proj, layer)]
                  + cast_in),
        out_specs=[pl.BlockSpec((tm, D), prev)] + cast_out,
        out_shape=[jax.ShapeDtypeStruct((T, D), F32)] + cast_shapes,
        scratch_shapes=[pltpu.VMEM((tm, D), F32), pltpu.VMEM((tm, D), BF16),
                        pltpu.VMEM((tm, D), F32), pltpu.VMEM((tm, D), BF16),
                        pltpu.VMEM((tm, d_ff), BF16),
                        pltpu.VMEM((n_d * MIX_A // LANE, tm, LANE), F32),
                        pltpu.VMEM((n_d, tm, LANE), F32),
                        pltpu.VMEM((tm, MIX_A), BF16)],
        compiler_params=_cparams(("arbitrary",)),
        name="mix_ffn_ple",
    )(x, *outs, *lses, ob, p, w_o, g2, w_in, w_out, gp, w_gate, w_proj, *[w for w, _ in casts])
    return res[0], res[1:]


def kernel(x, p, rel_bias, norm_ffn1, ffn1_w_in, ffn1_w_out, norm_mix, w_qkv, q_norm_a, k_norm_a, q_norm_b, k_norm_b, sink_b, w_o, norm_ffn2, ffn2_w_in, ffn2_w_out, norm_ple, w_ple_gate, w_ple_proj):
    B, S, D = x.shape
    depth = p.shape[0]
    T = B * S
    x = x.reshape(T, D)
    p = p.reshape(depth, T, p.shape[-1])

    bias_a = [_bias_tiles(rel_bias, window // (2 * d), d, 0) for window, d in DILATED_CONFIGS]
    bias_b = _bias_tiles(rel_bias, SWA_RADIUS, 1, N_HEADS)

    scale = HEAD_DIM ** -0.5 * LOG2E
    tile = lambda g, n: jnp.tile(g, (1, n))
    gains = jnp.concatenate([
        tile(q_norm_a * scale, N_HEADS), tile(k_norm_a, N_HEADS),
        jnp.ones((depth, MIX_A), F32),
        tile(q_norm_b * scale, N_HEADS), tile(k_norm_b, N_KV_B),
        jnp.ones((depth, KV_B), F32)], axis=1)[:, None]
    row = lambda g: g[:, None]
    first_half = (ffn1_w_in, ffn1_w_out, w_qkv)
    second_half = (w_o, ffn2_w_in, ffn2_w_out, w_ple_gate)
    w_first = [w[0].astype(BF16) for w in first_half]
    w_proj_b = w_ple_proj.astype(BF16)

    for i in range(depth):
        x, (qa, ka, va), (qb, kb, vb), w_second = _ffn_qkv(
            x, row(norm_ffn1), w_first[0], w_first[1], row(norm_mix), w_first[2], gains, i,
            [(w, i) for w in second_half])
        outs, lses = [], []
        for di, (window, d) in enumerate(DILATED_CONFIGS):
            o, lse = _attn_a(qa[di], ka[di], va[di], bias_a[di], B, window=window, dilation=d)
            outs.append(o)
            lses.append(lse)
        ob = _attn_b(qb, kb, vb, bias_b, sink_b[i], B)
        w_o_b, w_in2, w_out2, w_gate_b = w_second
        x, w_first = _mix_ffn_ple(
            x, outs, lses, ob, p, w_o_b, row(norm_ffn2), w_in2, w_out2, row(norm_ple), w_gate_b,
            w_proj_b, i, [(w, i + 1) for w in first_half] if i + 1 < depth else [])
    return x.reshape(B, S, D)
```

```python
import functools
import math

import jax
import jax.numpy as jnp
import numpy as np
from jax import lax
from jax.experimental import pallas as pl
from jax.experimental.pallas import tpu as pltpu

HEAD_DIM = 64
N_HEADS = 8
N_KV_B = 2
GROUP_B = N_HEADS // N_KV_B
MIX_A = N_HEADS * HEAD_DIM
KV_B = N_KV_B * HEAD_DIM
DILATED_CONFIGS = ((128, 1), (512, 4), (2048, 16))
DILATIONS = tuple(d for _, d in DILATED_CONFIGS)
SWA_RADIUS = 128
N_BUCKETS = 32
MAX_DISTANCE = 1024
EPS = 1e-6
NEG = -1e30
LOG2E = math.log2(math.e)
QBLOCK = 128
ATTN_GROUP = 16
LSE_LANES = 16
LANE = 128
BF16_TILE_ROWS = 16
FF_CHUNK = 256
ROW_TILE = 512

VMEM_LIMIT = 58 * 1024 * 1024

BF16 = jnp.bfloat16
F32 = jnp.float32


def _cparams(sem):
    return pltpu.CompilerParams(dimension_semantics=sem, vmem_limit_bytes=VMEM_LIMIT)


def _resident(shape, index_map):
    return pl.BlockSpec(shape, index_map, pipeline_mode=pl.Buffered(1))


def _layer_block(w, layer):
    return _resident((None,) + w.shape[1:], lambda i: (layer, 0, 0))


def _whole(w):
    return _resident(w.shape, lambda i: (0, 0))


def _cast_rows(rows, steps):
    r = next(r for r in range(BF16_TILE_ROWS, rows + 1, BF16_TILE_ROWS)
             if rows % r == 0 and rows // r <= steps)
    return r, rows // r


def _cast_plumbing(casts, steps):
    in_specs, out_specs, out_shapes = [], [], []
    for w, layer in casts:
        _, rows, cols = w.shape
        r, nblk = _cast_rows(rows, steps)
        in_specs.append(pl.BlockSpec((None, r, cols), functools.partial(
            lambda i, layer, nblk: (layer, jnp.minimum(i, nblk - 1), 0), layer=layer, nblk=nblk)))
        out_specs.append(pl.BlockSpec((r, cols), functools.partial(
            lambda i, nblk: (jnp.minimum(i, nblk - 1), 0), nblk=nblk)))
        out_shapes.append(jax.ShapeDtypeStruct((rows, cols), BF16))
    return in_specs, out_specs, out_shapes


def _cast_work(src_refs, dst_refs):
    def cast(src, dst):
        dst[...] = src[...].astype(BF16)
    return [functools.partial(cast, src, dst) for src, dst in zip(src_refs, dst_refs)]


def _rms_scale(x):
    return lax.rsqrt(jnp.mean(x * x, axis=-1, keepdims=True) + EPS)


def _ffn_hidden(h, win_ref, act_ref, fillers=()):
    d_ff = act_ref.shape[-1]
    chunks = range(0, d_ff, FF_CHUNK)
    fillers = list(fillers)
    per_chunk = -(-len(fillers) // len(chunks))
    for c in chunks:
        e = min(c + FF_CHUNK, d_ff)
        gate = jnp.dot(h, win_ref[:, c:e], preferred_element_type=F32)
        up = jnp.dot(h, win_ref[:, d_ff + c:d_ff + e], preferred_element_type=F32)
        act_ref[:, c:e] = (gate * jax.nn.sigmoid(gate) * up).astype(BF16)
        for fill in fillers[:per_chunk]:
            fill()
        del fillers[:per_chunk]


def _ffn_down(act_ref, wout_ref):
    return jnp.dot(act_ref[...], wout_ref[...], preferred_element_type=F32)


def _head_mean_sq(y):
    sq = y * y
    lo_half = lax.broadcasted_iota(jnp.int32, sq.shape, 1) < HEAD_DIM
    s_lo = jnp.sum(jnp.where(lo_half, sq, 0.0), axis=-1, keepdims=True)
    s_hi = jnp.sum(jnp.where(lo_half, 0.0, sq), axis=-1, keepdims=True)
    return jnp.where(lo_half, s_lo, s_hi) * (1.0 / HEAD_DIM)


def _qk_normed(y, gain):
    return y * lax.rsqrt(_head_mean_sq(y) + EPS) * gain


def _qkv_work(x_ref, g_ref, w_ref, gq_ref, a_refs, qb_ref, kb_ref, vb_ref, h_ref, scr_ref):
    n_d = len(DILATIONS)
    tm = x_ref.shape[0]
    n_blk = MIX_A // LANE

    def normalise():
        x = x_ref[...]
        h_ref[...] = (x * _rms_scale(x) * g_ref[...]).astype(BF16)

    def project_a(gi, normed):
        col = gi * MIX_A
        y = jnp.dot(h_ref[...], w_ref[:, col:col + MIX_A], preferred_element_type=F32)
        for c in range(n_blk):
            yc = y[:, c * LANE:(c + 1) * LANE]
            if normed:
                yc = _qk_normed(yc, gq_ref[:, col + c * LANE:col + (c + 1) * LANE])
            scr_ref[gi * n_blk + c] = yc

    def store_a(gi, di, d):
        out = a_refs[gi * n_d + di]
        for r in range(d):
            for c in range(n_blk):
                slab = scr_ref.at[gi * n_blk + c]
                rows = slab[...] if d == 1 else slab[pl.ds(r, tm // d, stride=d), :]
                out[:, r * MIX_A + c * LANE:r * MIX_A + (c + 1) * LANE] = rows.astype(BF16)

    def project_qb():
        col = 3 * MIX_A
        yq = jnp.dot(h_ref[...], w_ref[:, col:col + MIX_A], preferred_element_type=F32)
        for c in range(n_blk):
            cols = slice(c * LANE, (c + 1) * LANE)
            gain = gq_ref[:, col + c * LANE:col + (c + 1) * LANE]
            qb_ref[:, cols] = _qk_normed(yq[:, cols], gain).astype(BF16)

    def project_kvb():
        col = 4 * MIX_A
        ykv = jnp.dot(h_ref[...], w_ref[:, col:col + 2 * KV_B], preferred_element_type=F32)
        lane = lax.broadcasted_iota(jnp.int32, (tm, LANE), 1)
        for ref, t in ((kb_ref, _qk_normed(ykv[:, :KV_B], gq_ref[:, col:col + KV_B])),
                       (vb_ref, ykv[:, KV_B:])):
            swapped = pltpu.roll(t, HEAD_DIM, 1)
            ref[:, :LANE] = jnp.where(lane < HEAD_DIM, t, swapped).astype(BF16)
            ref[:, LANE:] = jnp.where(lane < HEAD_DIM, swapped, t).astype(BF16)

    work = [normalise]
    for gi, normed in enumerate((True, True, False)):
        work.append(functools.partial(project_a, gi, normed))
        work += [functools.partial(store_a, gi, di, d) for di, d in enumerate(DILATIONS)]
    return work + [project_qb, project_kvb]


def _ffn_qkv_kernel(x_ref, g1_ref, win_ref, wout_ref, gm_ref, wqkv_ref, gq_ref, *refs, n_cast):
    n_d = len(DILATIONS)
    cast_src, refs = refs[:n_cast], refs[n_cast:]
    x1_ref, a_refs, refs = refs[0], refs[1:1 + 3 * n_d], refs[1 + 3 * n_d:]
    (qb_ref, kb_ref, vb_ref), refs = refs[:3], refs[3:]
    cast_dst, (x1_carry, x1_work, act_ref, hm_ref, scr_ref) = refs[:n_cast], refs[n_cast:]
    i = pl.program_id(0)
    last = pl.num_programs(0) - 1
    cast_work = _cast_work(cast_src, cast_dst)

    def qkv_work(src_ref):
        return _qkv_work(src_ref, gm_ref, wqkv_ref, gq_ref, a_refs, qb_ref, kb_ref, vb_ref,
                         hm_ref, scr_ref)

    def stage1(fillers=()):
        x = x_ref[...]
        _ffn_hidden((x * _rms_scale(x) * g1_ref[...]).astype(BF16), win_ref, act_ref, fillers)
        x1 = x_ref[...] + 0.5 * _ffn_down(act_ref, wout_ref)
        x1_ref[...] = x1
        x1_carry[...] = x1

    @pl.when(i == 0)
    def _():
        stage1(cast_work)

    @pl.when(jnp.logical_and(i > 0, i < last))
    def _():
        x1_work[...] = x1_carry[...]
        stage1(qkv_work(x1_work) + cast_work)

    @pl.when(i == last)
    def _():
        for work in qkv_work(x1_carry) + cast_work:
            work()


def _ffn_qkv(x, g1, w_in, w_out, gm, w_qkv, gains, layer, casts, *, tm=ROW_TILE):
    T, D = x.shape
    nt = T // tm
    d_ff = w_out.shape[0]
    cast_in, cast_out, cast_shapes = _cast_plumbing(casts, nt + 1)
    cur = lambda i: (jnp.minimum(i, nt - 1), 0)
    prev = lambda i: (jnp.maximum(i - 1, 0), 0)
    gain = lambda width: _resident((None, 1, width), lambda i: (layer, 0, 0))
    a_shapes = [(d, MIX_A) for _ in range(3) for d in DILATIONS]
    b_widths = [MIX_A, N_KV_B * LANE, N_KV_B * LANE]
    shapes = [(T // d, d * c) for d, c in a_shapes] + [(T, c) for c in b_widths]
    blocks = [(tm // d, d * c) for d, c in a_shapes] + [(tm, c) for c in b_widths]
    outs = pl.pallas_call(
        functools.partial(_ffn_qkv_kernel, n_cast=len(casts)),
        grid=(nt + 1,),
        in_specs=[
            pl.BlockSpec((tm, D), cur), gain(D), _whole(w_in), _whole(w_out), gain(D),
            _whole(w_qkv), gain(gains.shape[-1]),
        ] + cast_in,
        out_specs=([pl.BlockSpec((tm, D), cur)] + [pl.BlockSpec(b, prev) for b in blocks]
                   + cast_out),
        out_shape=([jax.ShapeDtypeStruct((T, D), F32)]
                   + [jax.ShapeDtypeStruct(s, BF16) for s in shapes] + cast_shapes),
        scratch_shapes=[pltpu.VMEM((tm, D), F32), pltpu.VMEM((tm, D), F32),
                        pltpu.VMEM((tm, d_ff), BF16), pltpu.VMEM((tm, D), BF16),
                        pltpu.VMEM((3 * MIX_A // LANE, tm, LANE), F32)],
        compiler_params=_cparams(("arbitrary",)),
        name="ffn_qkv",
    )(x, g1, w_in, w_out, gm, w_qkv, gains, *[w for w, _ in casts])
    n_d = len(DILATIONS)
    x1, outs, cast = outs[0], outs[1:len(outs) - len(casts)], outs[len(outs) - len(casts):]
    return x1, (outs[:n_d], outs[n_d:2 * n_d], outs[2 * n_d:3 * n_d]), outs[3 * n_d:], cast


def _t5_bucket_np(rel):
    half = N_BUCKETS // 2
    max_exact = half // 2
    ret = np.where(rel > 0, half, 0)
    n = np.abs(rel)
    nf = np.maximum(n, 1).astype(np.float64)
    large = max_exact + (np.log(nf / max_exact) / math.log(MAX_DISTANCE / max_exact)
                         * (half - max_exact)).astype(np.int64)
    large = np.minimum(large, half - 1)
    return ret + np.where(n < max_exact, n, large)


def _bucket_tiles(radius, dilation):
    w = QBLOCK + 2 * radius
    key = np.arange(w)[:, None]
    q = np.arange(QBLOCK)[None, :]
    tiles = []
    for off in (0, -radius, -2 * radius):
        rel = off + key - q
        tiles.append(np.where(np.abs(rel) <= radius, _t5_bucket_np(rel * dilation), -1))
    return np.stack(tiles).astype(np.int32)


def _bias_kernel(rb_ref, idx_ref, o_ref, *, head0, buckets):
    idx = idx_ref[...]
    for h in range(N_HEADS):
        tile = jnp.full(idx.shape, NEG, F32)
        for b in buckets:
            tile = jnp.where(idx == b, rb_ref[b, head0 + h] * LOG2E, tile)
        o_ref[h] = tile


def _bias_tiles(rel_bias, radius, dilation, head0):
    idx_np = _bucket_tiles(radius, dilation)
    buckets = tuple(int(b) for b in np.unique(idx_np[idx_np >= 0]))
    nv, w, bq = idx_np.shape
    return pl.pallas_call(
        functools.partial(_bias_kernel, head0=head0, buckets=buckets),
        grid=(nv,),
        in_specs=[
            pl.BlockSpec(memory_space=pltpu.SMEM),
            pl.BlockSpec((None, w, bq), lambda v: (v, 0, 0)),
        ],
        out_specs=pl.BlockSpec((None, N_HEADS, w, bq), lambda v: (v, 0, 0, 0)),
        out_shape=jax.ShapeDtypeStruct((nv, N_HEADS, w, bq), F32),
        compiler_params=_cparams(("arbitrary",)),
        name="bias_tiles",
    )(rel_bias, jnp.asarray(idx_np))


def _window(i, nb, radius, length):
    w = QBLOCK + 2 * radius
    start = pl.multiple_of(jnp.clip(i * QBLOCK - radius, 0, length - w), radius)
    var = jnp.where(i == 0, 0, jnp.where(i == nb - 1, 2, 1))
    return start, var


_NT = (((1,), (1,)), ((), ()))
_TN = (((0,), (0,)), ((), ()))


def _pair_attention(qp, kp, vp, bias_lo, bias_hi, sink_lo=None, sink_hi=None):
    lane = lax.broadcasted_iota(jnp.int32, qp.shape, 1)
    zero = jnp.zeros_like(qp)
    vt = vp.T
    ones = jnp.ones((HEAD_DIM, vt.shape[1]), vt.dtype)
    outs, lses = [], []
    for hi, (bias, sink) in enumerate(((bias_lo, sink_lo), (bias_hi, sink_hi))):
        keep = (lane >= HEAD_DIM) if hi else (lane < HEAD_DIM)
        s = lax.dot_general(kp, jnp.where(keep, qp, zero), _NT, preferred_element_type=F32) + bias
        m = jnp.max(s, axis=0, keepdims=True)
        if sink is not None:
            m = jnp.maximum(m, sink)
        p = jnp.exp2(s - m).astype(BF16)
        vt_den = jnp.concatenate([ones, vt[HEAD_DIM:]] if hi else [vt[:HEAD_DIM], ones], axis=0)
        o = jnp.dot(vt_den, p, preferred_element_type=F32)
        den = o[:1] if hi else o[HEAD_DIM:HEAD_DIM + 1]
        if sink is not None:
            den = den + jnp.exp2(sink - m)
        outs.append((o[HEAD_DIM:] if hi else o[:HEAD_DIM]) * (1.0 / den))
        lses.append(m + jnp.log2(den))
    o_t = jnp.concatenate(outs, axis=0)
    return o_t.astype(BF16).T, lses[0], lses[1]


def _attn_a_kernel(q_ref, k_ref, v_ref, bias_ref, o_ref, lse_ref, *,
                   radius, length, nb, group, classes):
    w = QBLOCK + 2 * radius
    for cl, g in [(cl, g) for cl in range(classes) for g in range(group)]:
        start, var = _window(pl.program_id(2) * group + g, nb, radius, length)
        rows = slice(g * QBLOCK, (g + 1) * QBLOCK)
        lses = []
        for pr in range(N_HEADS // 2):
            cols = slice(cl * MIX_A + pr * LANE, cl * MIX_A + (pr + 1) * LANE)
            o, lse_lo, lse_hi = _pair_attention(
                q_ref[rows, cols], k_ref[pl.ds(start, w), cols], v_ref[pl.ds(start, w), cols],
                bias_ref[var, 2 * pr], bias_ref[var, 2 * pr + 1])
            o_ref[rows, cols] = o
            lses += [lse_lo, lse_hi]
        lse_t = jnp.concatenate([jnp.broadcast_to(l, (LSE_LANES, QBLOCK)) for l in lses], axis=0)
        lse_ref[rows, cl * LANE:(cl + 1) * LANE] = lse_t.T


def _attn_a(q, k, v, bias, batch, *, window, dilation):
    d = dilation
    L = q.shape[0] // batch
    nb = L // QBLOCK
    radius = window // (2 * d)
    group = min(nb, ATTN_GROUP)
    classes = min(d, ATTN_GROUP // group)
    seq = lambda t: t.reshape(batch, L, t.shape[-1])
    o, lse = pl.pallas_call(
        functools.partial(_attn_a_kernel, radius=radius, length=L, nb=nb, group=group,
                          classes=classes),
        grid=(batch, d // classes, nb // group),
        in_specs=[
            pl.BlockSpec((None, group * QBLOCK, classes * MIX_A), lambda b, r, i: (b, i, r)),
            pl.BlockSpec((None, L, classes * MIX_A), lambda b, r, i: (b, 0, r)),
            pl.BlockSpec((None, L, classes * MIX_A), lambda b, r, i: (b, 0, r)),
            _resident(bias.shape, lambda b, r, i: (0, 0, 0, 0)),
        ],
        out_specs=[
            pl.BlockSpec((None, group * QBLOCK, classes * MIX_A), lambda b, r, i: (b, i, r)),
            pl.BlockSpec((None, group * QBLOCK, classes * LANE), lambda b, r, i: (b, i, r)),
        ],
        out_shape=[
            jax.ShapeDtypeStruct((batch, L, d * MIX_A), BF16),
            jax.ShapeDtypeStruct((batch, L, d * LANE), F32),
        ],
        compiler_params=_cparams(("parallel", "parallel", "arbitrary")),
        name=f"attn_a_d{d}",
    )(seq(q), seq(k), seq(v), bias)
    return o.reshape(batch * L, d * MIX_A), lse.reshape(batch * L, d * LANE)


def _attn_b_kernel(sink_ref, q_ref, k_ref, v_ref, bias_ref, o_ref, *, radius, length, nb, group):
    w = QBLOCK + 2 * radius
    for g in range(group):
        start, var = _window(pl.program_id(1) * group + g, nb, radius, length)
        rows = slice(g * QBLOCK, (g + 1) * QBLOCK)
        for pr in range(N_HEADS // 2):
            cols = slice(pr * LANE, (pr + 1) * LANE)
            kv_cols = slice(2 * pr // GROUP_B * LANE, (2 * pr // GROUP_B + 1) * LANE)
            lo, hi = 2 * pr, 2 * pr + 1
            o, _, _ = _pair_attention(
                q_ref[rows, cols], k_ref[pl.ds(start, w), kv_cols], v_ref[pl.ds(start, w), kv_cols],
                bias_ref[var, lo], bias_ref[var, hi], sink_ref[lo] * LOG2E, sink_ref[hi] * LOG2E)
            o_ref[rows, cols] = o


def _attn_b(q, k, v, bias, sink, batch):
    S = q.shape[0] // batch
    nb = S // QBLOCK
    group = min(nb, ATTN_GROUP)
    seq = lambda t: t.reshape(batch, S, t.shape[-1])
    o = pl.pallas_call(
        functools.partial(_attn_b_kernel, radius=SWA_RADIUS, length=S, nb=nb, group=group),
        grid=(batch, nb // group),
        in_specs=[
            pl.BlockSpec(memory_space=pltpu.SMEM),
            pl.BlockSpec((None, group * QBLOCK, MIX_A), lambda b, i: (b, i, 0)),
            pl.BlockSpec((None, S, N_KV_B * LANE), lambda b, i: (b, 0, 0)),
            pl.BlockSpec((None, S, N_KV_B * LANE), lambda b, i: (b, 0, 0)),
            _resident(bias.shape, lambda b, i: (0, 0, 0, 0)),
        ],
        out_specs=pl.BlockSpec((None, group * QBLOCK, MIX_A), lambda b, i: (b, i, 0)),
        out_shape=jax.ShapeDtypeStruct((batch, S, MIX_A), BF16),
        compiler_params=_cparams(("parallel", "arbitrary")),
        name="attn_b",
    )(sink, seq(q), seq(k), seq(v), bias)
    return o.reshape(batch * S, MIX_A)


def _merge_work(o_refs, l_refs, oscr_ref, lscr_ref, mrg_ref):
    n_d = len(DILATIONS)
    tm = mrg_ref.shape[0]
    n_blk = MIX_A // LANE

    def unpermute(di, d, r):
        rows = slice(None) if d == 1 else pl.ds(r, tm // d, stride=d)
        lscr_ref[di, rows, :] = l_refs[di][:, r * LANE:(r + 1) * LANE]
        for c in range(n_blk):
            src = o_refs[di][:, r * MIX_A + c * LANE:r * MIX_A + (c + 1) * LANE]
            oscr_ref[di * n_blk + c, rows, :] = src.astype(F32)

    def weights():
        lse = [lscr_ref[di] for di in range(n_d)]
        mx = functools.reduce(jnp.maximum, lse)
        ex = [jnp.exp2(l - mx) for l in lse]
        inv = 1.0 / functools.reduce(jnp.add, ex)
        for di, e in enumerate(ex):
            lscr_ref[di] = e * inv

    def merge(c):
        lane = lax.broadcasted_iota(jnp.int32, (tm, LANE), 1)
        lo, hi = 2 * c * LSE_LANES, (2 * c + 1) * LSE_LANES
        merged = jnp.zeros((tm, LANE), F32)
        for di in range(n_d):
            wt = lscr_ref[di]
            wfull = jnp.where(lane < HEAD_DIM, wt[:, lo:lo + 1], wt[:, hi:hi + 1])
            merged = merged + wfull * oscr_ref[di * n_blk + c]
        mrg_ref[:, c * LANE:(c + 1) * LANE] = merged.astype(BF16)

    work = [functools.partial(unpermute, di, d, r)
            for di, d in enumerate(DILATIONS) for r in range(d)]
    return work + [weights] + [functools.partial(merge, c) for c in range(n_blk)]


def _mix_ffn_ple_kernel(x_ref, *refs, n_cast):
    n_d = len(DILATIONS)
    o_refs, l_refs, refs = refs[:n_d], refs[n_d:2 * n_d], refs[2 * n_d:]
    (ob_ref, p_ref, wo_ref, g2_ref, win_ref, wout_ref, gp_ref, wg_ref, wp_ref), refs = (
        refs[:9], refs[9:])
    cast_src, out_ref, cast_dst, refs = (refs[:n_cast], refs[n_cast],
                                         refs[n_cast + 1:2 * n_cast + 1], refs[2 * n_cast + 1:])
    x2_carry, h_carry, x2_work, h_work, act_ref, oscr_ref, lscr_ref, mrg_ref = refs

    i = pl.program_id(0)
    last = pl.num_programs(0) - 1
    cast_work = _cast_work(cast_src, cast_dst)

    def merge_work():
        return _merge_work(o_refs, l_refs, oscr_ref, lscr_ref, mrg_ref)

    def project():
        x2 = (x_ref[...]
              + jnp.dot(mrg_ref[...], wo_ref[:MIX_A, :], preferred_element_type=F32)
              + jnp.dot(ob_ref[...], wo_ref[MIX_A:, :], preferred_element_type=F32))
        x2_carry[...] = x2
        h_carry[...] = (x2 * _rms_scale(x2) * g2_ref[...]).astype(BF16)

    def stage2(x2_ref, h_ref, fillers=(), between=lambda: None):
        _ffn_hidden(h_ref[...], win_ref, act_ref, fillers)
        x3 = x2_ref[...] + 0.5 * _ffn_down(act_ref, wout_ref)
        hp = (x3 * _rms_scale(x3) * gp_ref[...]).astype(BF16)
        between()
        gate = jax.nn.sigmoid(jnp.dot(hp, wg_ref[...], preferred_element_type=F32))
        proj = jnp.dot(p_ref[...].astype(BF16), wp_ref[...], preferred_element_type=F32)
        out_ref[...] = x3 + gate * proj

    @pl.when(i == 0)
    def _():
        for work in merge_work() + cast_work:
            work()
        project()

    @pl.when(jnp.logical_and(i > 0, i < last))
    def _():
        x2_work[...] = x2_carry[...]
        h_work[...] = h_carry[...]
        stage2(x2_work, h_work, merge_work() + cast_work, project)

    @pl.when(i == last)
    def _():
        stage2(x2_carry, h_carry, cast_work)


def _mix_ffn_ple(x, outs, lses, ob, p, w_o, g2, w_in, w_out, gp, w_gate, w_proj, layer, casts,
                 *, tm=ROW_TILE):
    T, D = x.shape
    nt = T // tm
    d_ff = w_out.shape[0]
    n_d = len(DILATIONS)
    cast_in, cast_out, cast_shapes = _cast_plumbing(casts, nt + 1)
    cur = lambda i: (jnp.minimum(i, nt - 1), 0)
    prev = lambda i: (jnp.maximum(i - 1, 0), 0)
    gain = lambda: _resident((None, 1, D), lambda i: (layer, 0, 0))
    res = pl.pallas_call(
        functools.partial(_mix_ffn_ple_kernel, n_cast=len(casts)),
        grid=(nt + 1,),
        in_specs=([pl.BlockSpec((tm, D), cur)]
                  + [pl.BlockSpec((tm // d, d * MIX_A), cur) for d in DILATIONS]
                  + [pl.BlockSpec((tm // d, d * LANE), cur) for d in DILATIONS]
                  + [pl.BlockSpec((tm, MIX_A), cur),
                     pl.BlockSpec((None, tm, p.shape[-1]),
                                  lambda i: (layer, jnp.maximum(i - 1, 0), 0)),
                     _whole(w_o), gain(), _whole(w_in), _whole(w_out), gain(), _whole(w_gate),
                     _layer_block(w_proj, layer)]
                  + cast_in),
        out_specs=[pl.BlockSpec((tm, D), prev)] + cast_out,
        out_shape=[jax.ShapeDtypeStruct((T, D), F32)] + cast_shapes,
        scratch_shapes=[pltpu.VMEM((tm, D), F32), pltpu.VMEM((tm, D), BF16),
                        pltpu.VMEM((tm, D), F32), pltpu.VMEM((tm, D), BF16),
                        pltpu.VMEM((tm, d_ff), BF16),
                        pltpu.VMEM((n_d * MIX_A // LANE, tm, LANE), F32),
                        pltpu.VMEM((n_d, tm, LANE), F32),
                        pltpu.VMEM((tm, MIX_A), BF16)],
        compiler_params=_cparams(("arbitrary",)),
        name="mix_ffn_ple",
    )(x, *outs, *lses, ob, p, w_o, g2, w_in, w_out, gp, w_gate, w_proj, *[w for w, _ in casts])
    return res[0], res[1:]


def kernel(x, p, rel_bias, norm_ffn1, ffn1_w_in, ffn1_w_out, norm_mix, w_qkv, q_norm_a, k_norm_a, q_norm_b, k_norm_b, sink_b, w_o, norm_ffn2, ffn2_w_in, ffn2_w_out, norm_ple, w_ple_gate, w_ple_proj):
    B, S, D = x.shape
    depth = p.shape[0]
    T = B * S
    x = x.reshape(T, D)
    p = p.reshape(depth, T, p.shape[-1])

    bias_a = [_bias_tiles(rel_bias, window // (2 * d), d, 0) for window, d in DILATED_CONFIGS]
    bias_b = _bias_tiles(rel_bias, SWA_RADIUS, 1, N_HEADS)

    scale = HEAD_DIM ** -0.5 * LOG2E
    tile = lambda g, n: jnp.tile(g, (1, n))
    gains = jnp.concatenate([
        tile(q_norm_a * scale, N_HEADS), tile(k_norm_a, N_HEADS),
        jnp.ones((depth, MIX_A), F32),
        tile(q_norm_b * scale, N_HEADS), tile(k_norm_b, N_KV_B),
        jnp.ones((depth, KV_B), F32)], axis=1)[:, None]
    row = lambda g: g[:, None]
    first_half = (ffn1_w_in, ffn1_w_out, w_qkv)
    second_half = (w_o, ffn2_w_in, ffn2_w_out, w_ple_gate)
    w_first = [w[0].astype(BF16) for w in first_half]
    w_proj_b = w_ple_proj.astype(BF16)

    for i in range(depth):
        x, (qa, ka, va), (qb, kb, vb), w_second = _ffn_qkv(
            x, row(norm_ffn1), w_first[0], w_first[1], row(norm_mix), w_first[2], gains, i,
            [(w, i) for w in second_half])
        outs, lses = [], []
        for di, (window, d) in enumerate(DILATED_CONFIGS):
            o, lse = _attn_a(qa[di], ka[di], va[di], bias_a[di], B, window=window, dilation=d)
            outs.append(o)
            lses.append(lse)
        ob = _attn_b(qb, kb, vb, bias_b, sink_b[i], B)
        w_o_b, w_in2, w_out2, w_gate_b = w_second
        x, w_first = _mix_ffn_ple(
            x, outs, lses, ob, p, w_o_b, row(norm_ffn2), w_in2, w_out2, row(norm_ple), w_gate_b,
            w_proj_b, i, [(w, i + 1) for w in first_half] if i + 1 < depth else [])
    return x.reshape(B, S, D)
```

```python
import functools
import math

import jax
import jax.numpy as jnp
import numpy as np
from jax import lax
from jax.experimental import pallas as pl
from jax.experimental.pallas import tpu as pltpu

HEAD_DIM = 64
N_HEADS = 8
N_KV_B = 2
GROUP_B = N_HEADS // N_KV_B
MIX_A = N_HEADS * HEAD_DIM
KV_B = N_KV_B * HEAD_DIM
DILATED_CONFIGS = ((128, 1), (512, 4), (2048, 16))
DILATIONS = tuple(d for _, d in DILATED_CONFIGS)
SWA_RADIUS = 128
N_BUCKETS = 32
MAX_DISTANCE = 1024
EPS = 1e-6
NEG = -1e30
LOG2E = math.log2(math.e)
QBLOCK = 128
ATTN_GROUP = 16
LSE_LANES = 16
LANE = 128
BF16_TILE_ROWS = 16
FF_CHUNK = 256
ROW_TILE = 512

VMEM_LIMIT = 58 * 1024 * 1024

BF16 = jnp.bfloat16
F32 = jnp.float32


def _cparams(sem):
    return pltpu.CompilerParams(dimension_semantics=sem, vmem_limit_bytes=VMEM_LIMIT)


def _resident(shape, index_map):
    return pl.BlockSpec(shape, index_map, pipeline_mode=pl.Buffered(1))


def _layer_block(w, layer):
    return _resident((None,) + w.shape[1:], lambda i: (layer, 0, 0))


def _whole(w):
    return _resident(w.shape, lambda i: (0, 0))


def _cast_rows(rows, steps):
    r = next(r for r in range(BF16_TILE_ROWS, rows + 1, BF16_TILE_ROWS)
             if rows % r == 0 and rows // r <= steps)
    return r, rows // r


def _cast_plumbing(casts, steps):
    in_specs, out_specs, out_shapes = [], [], []
    for w, layer in casts:
        _, rows, cols = w.shape
        r, nblk = _cast_rows(rows, steps)
        in_specs.append(pl.BlockSpec((None, r, cols), functools.partial(
            lambda i, layer, nblk: (layer, jnp.minimum(i, nblk - 1), 0), layer=layer, nblk=nblk)))
        out_specs.append(pl.BlockSpec((r, cols), functools.partial(
            lambda i, nblk: (jnp.minimum(i, nblk - 1), 0), nblk=nblk)))
        out_shapes.append(jax.ShapeDtypeStruct((rows, cols), BF16))
    return in_specs, out_specs, out_shapes


def _cast_work(src_refs, dst_refs):
    def cast(src, dst):
        dst[...] = src[...].astype(BF16)
    return [functools.partial(cast, src, dst) for src, dst in zip(src_refs, dst_refs)]


def _rms_scale(x):
    return lax.rsqrt(jnp.mean(x * x, axis=-1, keepdims=True) + EPS)


def _ffn_hidden(h, win_ref, act_ref, fillers=()):
    d_ff = act_ref.shape[-1]
    chunks = range(0, d_ff, FF_CHUNK)
    fillers = list(fillers)
    per_chunk = -(-len(fillers) // len(chunks))
    for c in chunks:
        e = min(c + FF_CHUNK, d_ff)
        gate = jnp.dot(h, win_ref[:, c:e], preferred_element_type=F32)
        up = jnp.dot(h, win_ref[:, d_ff + c:d_ff + e], preferred_element_type=F32)
        act_ref[:, c:e] = (gate * jax.nn.sigmoid(gate) * up).astype(BF16)
        for fill in fillers[:per_chunk]:
            fill()
        del fillers[:per_chunk]


def _ffn_down(act_ref, wout_ref):
    return jnp.dot(act_ref[...], wout_ref[...], preferred_element_type=F32)


def _head_mean_sq(y):
    sq = y * y
    lo_half = lax.broadcasted_iota(jnp.int32, sq.shape, 1) < HEAD_DIM
    s_lo = jnp.sum(jnp.where(lo_half, sq, 0.0), axis=-1, keepdims=True)
    s_hi = jnp.sum(jnp.where(lo_half, 0.0, sq), axis=-1, keepdims=True)
    return jnp.where(lo_half, s_lo, s_hi) * (1.0 / HEAD_DIM)


def _qk_normed(y, gain):
    return y * lax.rsqrt(_head_mean_sq(y) + EPS) * gain


def _qkv_work(x_ref, g_ref, w_ref, gq_ref, a_refs, qb_ref, kb_ref, vb_ref, h_ref, scr_ref):
    n_d = len(DILATIONS)
    tm = x_ref.shape[0]
    n_blk = MIX_A // LANE

    def normalise():
        x = x_ref[...]
        h_ref[...] = (x * _rms_scale(x) * g_ref[...]).astype(BF16)

    def project_a(gi, normed):
        col = gi * MIX_A
        y = jnp.dot(h_ref[...], w_ref[:, col:col + MIX_A], preferred_element_type=F32)
        for c in range(n_blk):
            yc = y[:, c * LANE:(c + 1) * LANE]
            if normed:
                yc = _qk_normed(yc, gq_ref[:, col + c * LANE:col + (c + 1) * LANE])
            scr_ref[gi * n_blk + c] = yc

    def store_a(gi, di, d):
        out = a_refs[gi * n_d + di]
        for r in range(d):
            for c in range(n_blk):
                slab = scr_ref.at[gi * n_blk + c]
                rows = slab[...] if d == 1 else slab[pl.ds(r, tm // d, stride=d), :]
                out[:, r * MIX_A + c * LANE:r * MIX_A + (c + 1) * LANE] = rows.astype(BF16)

    def project_qb():
        col = 3 * MIX_A
        yq = jnp.dot(h_ref[...], w_ref[:, col:col + MIX_A], preferred_element_type=F32)
        for c in range(n_blk):
            cols = slice(c * LANE, (c + 1) * LANE)
            gain = gq_ref[:, col + c * LANE:col + (c + 1) * LANE]
            qb_ref[:, cols] = _qk_normed(yq[:, cols], gain).astype(BF16)

    def project_kvb():
        col = 4 * MIX_A
        ykv = jnp.dot(h_ref[...], w_ref[:, col:col + 2 * KV_B], preferred_element_type=F32)
        lane = lax.broadcasted_iota(jnp.int32, (tm, LANE), 1)
        for ref, t in ((kb_ref, _qk_normed(ykv[:, :KV_B], gq_ref[:, col:col + KV_B])),
                       (vb_ref, ykv[:, KV_B:])):
            swapped = pltpu.roll(t, HEAD_DIM, 1)
            ref[:, :LANE] = jnp.where(lane < HEAD_DIM, t, swapped).astype(BF16)
            ref[:, LANE:] = jnp.where(lane < HEAD_DIM, swapped, t).astype(BF16)

    work = [normalise]
    for gi, normed in enumerate((True, True, False)):
        work.append(functools.partial(project_a, gi, normed))
        work += [functools.partial(store_a, gi, di, d) for di, d in enumerate(DILATIONS)]
    return work + [project_qb, project_kvb]


def _ffn_qkv_kernel(x_ref, g1_ref, win_ref, wout_ref, gm_ref, wqkv_ref, gq_ref, *refs, n_cast):
    n_d = len(DILATIONS)
    cast_src, refs = refs[:n_cast], refs[n_cast:]
    x1_ref, qkv_a_refs, qkv_b_ref, refs = refs[0], refs[1:1 + n_d], refs[1 + n_d], refs[2 + n_d:]
    cast_dst, (x1_carry, x1_work, act_ref, hm_ref, scr_ref) = refs[:n_cast], refs[n_cast:]
    a_refs = [qkv_a_refs[di].at[:, gi * d * MIX_A:(gi + 1) * d * MIX_A]
              for gi in range(3) for di, d in enumerate(DILATIONS)]
    qb_ref = qkv_b_ref.at[:, :MIX_A]
    kb_ref = qkv_b_ref.at[:, MIX_A:MIX_A + N_KV_B * LANE]
    vb_ref = qkv_b_ref.at[:, MIX_A + N_KV_B * LANE:]
    i = pl.program_id(0)
    last = pl.num_programs(0) - 1
    cast_work = _cast_work(cast_src, cast_dst)

    def qkv_work(src_ref):
        return _qkv_work(src_ref, gm_ref, wqkv_ref, gq_ref, a_refs, qb_ref, kb_ref, vb_ref,
                         hm_ref, scr_ref)

    def stage1(fillers=()):
        x = x_ref[...]
        _ffn_hidden((x * _rms_scale(x) * g1_ref[...]).astype(BF16), win_ref, act_ref, fillers)
        x1 = x_ref[...] + 0.5 * _ffn_down(act_ref, wout_ref)
        x1_ref[...] = x1
        x1_carry[...] = x1

    @pl.when(i == 0)
    def _():
        stage1(cast_work)

    @pl.when(jnp.logical_and(i > 0, i < last))
    def _():
        x1_work[...] = x1_carry[...]
        stage1(qkv_work(x1_work) + cast_work)

    @pl.when(i == last)
    def _():
        for work in qkv_work(x1_carry) + cast_work:
            work()


def _ffn_qkv(x, g1, w_in, w_out, gm, w_qkv, gains, layer, casts, *, tm=ROW_TILE):
    T, D = x.shape
    nt = T // tm
    d_ff = w_out.shape[0]
    cast_in, cast_out, cast_shapes = _cast_plumbing(casts, nt + 1)
    cur = lambda i: (jnp.minimum(i, nt - 1), 0)
    prev = lambda i: (jnp.maximum(i - 1, 0), 0)
    gain = lambda width: _resident((None, 1, width), lambda i: (layer, 0, 0))
    b_width = MIX_A + 2 * N_KV_B * LANE
    shapes = [(T // d, 3 * d * MIX_A) for d in DILATIONS] + [(T, b_width)]
    blocks = [(tm // d, 3 * d * MIX_A) for d in DILATIONS] + [(tm, b_width)]
    outs = pl.pallas_call(
        functools.partial(_ffn_qkv_kernel, n_cast=len(casts)),
        grid=(nt + 1,),
        in_specs=[
            pl.BlockSpec((tm, D), cur), gain(D), _whole(w_in), _whole(w_out), gain(D),
            _whole(w_qkv), gain(gains.shape[-1]),
        ] + cast_in,
        out_specs=([pl.BlockSpec((tm, D), cur)] + [pl.BlockSpec(b, prev) for b in blocks]
                   + cast_out),
        out_shape=([jax.ShapeDtypeStruct((T, D), F32)]
                   + [jax.ShapeDtypeStruct(s, BF16) for s in shapes] + cast_shapes),
        scratch_shapes=[pltpu.VMEM((tm, D), F32), pltpu.VMEM((tm, D), F32),
                        pltpu.VMEM((tm, d_ff), BF16), pltpu.VMEM((tm, D), BF16),
                        pltpu.VMEM((3 * MIX_A // LANE, tm, LANE), F32)],
        compiler_params=_cparams(("arbitrary",)),
        name="ffn_qkv",
    )(x, g1, w_in, w_out, gm, w_qkv, gains, *[w for w, _ in casts])
    n_d = len(DILATIONS)
    return outs[0], outs[1:1 + n_d], outs[1 + n_d], outs[2 + n_d:]


def _t5_bucket_np(rel):
    half = N_BUCKETS // 2
    max_exact = half // 2
    ret = np.where(rel > 0, half, 0)
    n = np.abs(rel)
    nf = np.maximum(n, 1).astype(np.float64)
    large = max_exact + (np.log(nf / max_exact) / math.log(MAX_DISTANCE / max_exact)
                         * (half - max_exact)).astype(np.int64)
    large = np.minimum(large, half - 1)
    return ret + np.where(n < max_exact, n, large)


def _bucket_tiles(radius, dilation):
    w = QBLOCK + 2 * radius
    key = np.arange(w)[:, None]
    q = np.arange(QBLOCK)[None, :]
    tiles = []
    for off in (0, -radius, -2 * radius):
        rel = off + key - q
        tiles.append(np.where(np.abs(rel) <= radius, _t5_bucket_np(rel * dilation), -1))
    return np.stack(tiles).astype(np.int32)


def _bias_kernel(rb_ref, idx_ref, o_ref, *, head0, buckets):
    idx = idx_ref[...]
    for h in range(N_HEADS):
        tile = jnp.full(idx.shape, NEG, F32)
        for b in buckets:
            tile = jnp.where(idx == b, rb_ref[b, head0 + h] * LOG2E, tile)
        o_ref[h] = tile


def _bias_tiles(rel_bias, radius, dilation, head0):
    idx_np = _bucket_tiles(radius, dilation)
    buckets = tuple(int(b) for b in np.unique(idx_np[idx_np >= 0]))
    nv, w, bq = idx_np.shape
    return pl.pallas_call(
        functools.partial(_bias_kernel, head0=head0, buckets=buckets),
        grid=(nv,),
        in_specs=[
            pl.BlockSpec(memory_space=pltpu.SMEM),
            pl.BlockSpec((None, w, bq), lambda v: (v, 0, 0)),
        ],
        out_specs=pl.BlockSpec((None, N_HEADS, w, bq), lambda v: (v, 0, 0, 0)),
        out_shape=jax.ShapeDtypeStruct((nv, N_HEADS, w, bq), F32),
        compiler_params=_cparams(("arbitrary",)),
        name="bias_tiles",
    )(rel_bias, jnp.asarray(idx_np))


def _window(i, nb, radius, length):
    w = QBLOCK + 2 * radius
    start = pl.multiple_of(jnp.clip(i * QBLOCK - radius, 0, length - w), radius)
    var = jnp.where(i == 0, 0, jnp.where(i == nb - 1, 2, 1))
    return start, var


_NT = (((1,), (1,)), ((), ()))
_TN = (((0,), (0,)), ((), ()))


def _pair_attention(qp, kp, vp, bias_lo, bias_hi, sink_lo=None, sink_hi=None):
    lane = lax.broadcasted_iota(jnp.int32, qp.shape, 1)
    zero = jnp.zeros_like(qp)
    vt = vp.T
    ones = jnp.ones((HEAD_DIM, vt.shape[1]), vt.dtype)
    outs, lses = [], []
    for hi, (bias, sink) in enumerate(((bias_lo, sink_lo), (bias_hi, sink_hi))):
        keep = (lane >= HEAD_DIM) if hi else (lane < HEAD_DIM)
        s = lax.dot_general(kp, jnp.where(keep, qp, zero), _NT, preferred_element_type=F32) + bias
        m = jnp.max(s, axis=0, keepdims=True)
        if sink is not None:
            m = jnp.maximum(m, sink)
        p = jnp.exp2(s - m).astype(BF16)
        vt_den = jnp.concatenate([ones, vt[HEAD_DIM:]] if hi else [vt[:HEAD_DIM], ones], axis=0)
        o = jnp.dot(vt_den, p, preferred_element_type=F32)
        den = o[:1] if hi else o[HEAD_DIM:HEAD_DIM + 1]
        if sink is not None:
            den = den + jnp.exp2(sink - m)
        outs.append((o[HEAD_DIM:] if hi else o[:HEAD_DIM]) * (1.0 / den))
        lses.append(m + jnp.log2(den))
    o_t = jnp.concatenate(outs, axis=0)
    return o_t.astype(BF16).T, lses[0], lses[1]


def _attn_a_kernel(q_ref, k_ref, v_ref, bias_ref, o_ref, lse_ref, *,
                   radius, length, nb, group, classes):
    w = QBLOCK + 2 * radius
    for cl, g in [(cl, g) for cl in range(classes) for g in range(group)]:
        start, var = _window(pl.program_id(2) * group + g, nb, radius, length)
        rows = slice(g * QBLOCK, (g + 1) * QBLOCK)
        lses = []
        for pr in range(N_HEADS // 2):
            cols = slice(cl * MIX_A + pr * LANE, cl * MIX_A + (pr + 1) * LANE)
            o, lse_lo, lse_hi = _pair_attention(
                q_ref[rows, cols], k_ref[pl.ds(start, w), cols], v_ref[pl.ds(start, w), cols],
                bias_ref[var, 2 * pr], bias_ref[var, 2 * pr + 1])
            o_ref[rows, cols] = o
            lses += [lse_lo, lse_hi]
        lse_t = jnp.concatenate([jnp.broadcast_to(l, (LSE_LANES, QBLOCK)) for l in lses], axis=0)
        lse_ref[rows, cl * LANE:(cl + 1) * LANE] = lse_t.T


def _attn_a(qkv, bias, batch, *, window, dilation):
    d = dilation
    L = qkv.shape[0] // batch
    nb = L // QBLOCK
    radius = window // (2 * d)
    group = min(nb, ATTN_GROUP)
    classes = min(d, ATTN_GROUP // group)
    n_cg = d // classes
    qkv = qkv.reshape(batch, L, qkv.shape[-1])
    o, lse = pl.pallas_call(
        functools.partial(_attn_a_kernel, radius=radius, length=L, nb=nb, group=group,
                          classes=classes),
        grid=(batch, n_cg, nb // group),
        in_specs=[
            pl.BlockSpec((None, group * QBLOCK, classes * MIX_A), lambda b, r, i: (b, i, r)),
            pl.BlockSpec((None, L, classes * MIX_A), lambda b, r, i: (b, 0, n_cg + r)),
            pl.BlockSpec((None, L, classes * MIX_A), lambda b, r, i: (b, 0, 2 * n_cg + r)),
            _resident(bias.shape, lambda b, r, i: (0, 0, 0, 0)),
        ],
        out_specs=[
            pl.BlockSpec((None, group * QBLOCK, classes * MIX_A), lambda b, r, i: (b, i, r)),
            pl.BlockSpec((None, group * QBLOCK, classes * LANE), lambda b, r, i: (b, i, r)),
        ],
        out_shape=[
            jax.ShapeDtypeStruct((batch, L, d * MIX_A), BF16),
            jax.ShapeDtypeStruct((batch, L, d * LANE), F32),
        ],
        compiler_params=_cparams(("parallel", "parallel", "arbitrary")),
        name=f"attn_a_d{d}",
    )(qkv, qkv, qkv, bias)
    return o.reshape(batch * L, d * MIX_A), lse.reshape(batch * L, d * LANE)


def _attn_b_kernel(sink_ref, q_ref, k_ref, v_ref, bias_ref, o_ref, *, radius, length, nb, group):
    w = QBLOCK + 2 * radius
    for g in range(group):
        start, var = _window(pl.program_id(1) * group + g, nb, radius, length)
        rows = slice(g * QBLOCK, (g + 1) * QBLOCK)
        for pr in range(N_HEADS // 2):
            cols = slice(pr * LANE, (pr + 1) * LANE)
            kv_cols = slice(2 * pr // GROUP_B * LANE, (2 * pr // GROUP_B + 1) * LANE)
            lo, hi = 2 * pr, 2 * pr + 1
            o, _, _ = _pair_attention(
                q_ref[rows, cols], k_ref[pl.ds(start, w), kv_cols], v_ref[pl.ds(start, w), kv_cols],
                bias_ref[var, lo], bias_ref[var, hi], sink_ref[lo] * LOG2E, sink_ref[hi] * LOG2E)
            o_ref[rows, cols] = o


def _attn_b(qkv, bias, sink, batch):
    S = qkv.shape[0] // batch
    nb = S // QBLOCK
    group = min(nb, ATTN_GROUP)
    kv_width = N_KV_B * LANE
    k_block = MIX_A // kv_width
    qkv = qkv.reshape(batch, S, qkv.shape[-1])
    o = pl.pallas_call(
        functools.partial(_attn_b_kernel, radius=SWA_RADIUS, length=S, nb=nb, group=group),
        grid=(batch, nb // group),
        in_specs=[
            pl.BlockSpec(memory_space=pltpu.SMEM),
            pl.BlockSpec((None, group * QBLOCK, MIX_A), lambda b, i: (b, i, 0)),
            pl.BlockSpec((None, S, kv_width), lambda b, i: (b, 0, k_block)),
            pl.BlockSpec((None, S, kv_width), lambda b, i: (b, 0, k_block + 1)),
            _resident(bias.shape, lambda b, i: (0, 0, 0, 0)),
        ],
        out_specs=pl.BlockSpec((None, group * QBLOCK, MIX_A), lambda b, i: (b, i, 0)),
        out_shape=jax.ShapeDtypeStruct((batch, S, MIX_A), BF16),
        compiler_params=_cparams(("parallel", "arbitrary")),
        name="attn_b",
    )(sink, qkv, qkv, qkv, bias)
    return o.reshape(batch * S, MIX_A)


def _merge_work(o_refs, l_refs, oscr_ref, lscr_ref, mrg_ref):
    n_d = len(DILATIONS)
    tm = mrg_ref.shape[0]
    n_blk = MIX_A // LANE

    def unpermute(di, d, r):
        rows = slice(None) if d == 1 else pl.ds(r, tm // d, stride=d)
        lscr_ref[di, rows, :] = l_refs[di][:, r * LANE:(r + 1) * LANE]
        for c in range(n_blk):
            src = o_refs[di][:, r * MIX_A + c * LANE:r * MIX_A + (c + 1) * LANE]
            oscr_ref[di * n_blk + c, rows, :] = src.astype(F32)

    def weights():
        lse = [lscr_ref[di] for di in range(n_d)]
        mx = functools.reduce(jnp.maximum, lse)
        ex = [jnp.exp2(l - mx) for l in lse]
        inv = 1.0 / functools.reduce(jnp.add, ex)
        for di, e in enumerate(ex):
            lscr_ref[di] = e * inv

    def merge(c):
        lane = lax.broadcasted_iota(jnp.int32, (tm, LANE), 1)
        lo, hi = 2 * c * LSE_LANES, (2 * c + 1) * LSE_LANES
        merged = jnp.zeros((tm, LANE), F32)
        for di in range(n_d):
            wt = lscr_ref[di]
            wfull = jnp.where(lane < HEAD_DIM, wt[:, lo:lo + 1], wt[:, hi:hi + 1])
            merged = merged + wfull * oscr_ref[di * n_blk + c]
        mrg_ref[:, c * LANE:(c + 1) * LANE] = merged.astype(BF16)

    work = [functools.partial(unpermute, di, d, r)
            for di, d in enumerate(DILATIONS) for r in range(d)]
    return work + [weights] + [functools.partial(merge, c) for c in range(n_blk)]


def _mix_ffn_ple_kernel(x_ref, *refs, n_cast):
    n_d = len(DILATIONS)
    o_refs, l_refs, refs = refs[:n_d], refs[n_d:2 * n_d], refs[2 * n_d:]
    (ob_ref, p_ref, wo_ref, g2_ref, win_ref, wout_ref, gp_ref, wg_ref, wp_ref), refs = (
        refs[:9], refs[9:])
    cast_src, out_ref, cast_dst, refs = (refs[:n_cast], refs[n_cast],
                                         refs[n_cast + 1:2 * n_cast + 1], refs[2 * n_cast + 1:])
    x2_carry, h_carry, x2_work, h_work, act_ref, oscr_ref, lscr_ref, mrg_ref = refs

    i = pl.program_id(0)
    last = pl.num_programs(0) - 1
    cast_work = _cast_work(cast_src, cast_dst)

    def merge_work():
        return _merge_work(o_refs, l_refs, oscr_ref, lscr_ref, mrg_ref)

    def project():
        x2 = (x_ref[...]
              + jnp.dot(mrg_ref[...], wo_ref[:MIX_A, :], preferred_element_type=F32)
              + jnp.dot(ob_ref[...], wo_ref[MIX_A:, :], preferred_element_type=F32))
        x2_carry[...] = x2
        h_carry[...] = (x2 * _rms_scale(x2) * g2_ref[...]).astype(BF16)

    def stage2(x2_ref, h_ref, fillers=(), between=lambda: None):
        _ffn_hidden(h_ref[...], win_ref, act_ref, fillers)
        x3 = x2_ref[...] + 0.5 * _ffn_down(act_ref, wout_ref)
        hp = (x3 * _rms_scale(x3) * gp_ref[...]).astype(BF16)
        between()
        gate = jax.nn.sigmoid(jnp.dot(hp, wg_ref[...], preferred_element_type=F32))
        proj = jnp.dot(p_ref[...].astype(BF16), wp_ref[...], preferred_element_type=F32)
        out_ref[...] = x3 + gate * proj

    @pl.when(i == 0)
    def _():
        for work in merge_work() + cast_work:
            work()
        project()

    @pl.when(jnp.logical_and(i > 0, i < last))
    def _():
        x2_work[...] = x2_carry[...]
        h_work[...] = h_carry[...]
        stage2(x2_work, h_work, merge_work() + cast_work, project)

    @pl.when(i == last)
    def _():
        stage2(x2_carry, h_carry, cast_work)


def _mix_ffn_ple(x, outs, lses, ob, p, w_o, g2, w_in, w_out, gp, w_gate, w_proj, layer, casts,
                 *, tm=ROW_TILE):
    T, D = x.shape
    nt = T // tm
    d_ff = w_out.shape[0]
    n_d = len(DILATIONS)
    cast_in, cast_out, cast_shapes = _cast_plumbing(casts, nt + 1)
    cur = lambda i: (jnp.minimum(i, nt - 1), 0)
    prev = lambda i: (jnp.maximum(i - 1, 0), 0)
    gain = lambda: _resident((None, 1, D), lambda i: (layer, 0, 0))
    res = pl.pallas_call(
        functools.partial(_mix_ffn_ple_kernel, n_cast=len(casts)),
        grid=(nt + 1,),
        in_specs=([pl.BlockSpec((tm, D), cur)]
                  + [pl.BlockSpec((tm // d, d * MIX_A), cur) for d in DILATIONS]
                  + [pl.BlockSpec((tm // d, d * LANE), cur) for d in DILATIONS]
                  + [pl.BlockSpec((tm, MIX_A), cur),
                     pl.BlockSpec((None, tm, p.shape[-1]),
                                  lambda i: (layer, jnp.maximum(i - 1, 0), 0)),
                     _whole(w_o), gain(), _whole(w_in), _whole(w_out), gain(), _whole(w_gate),
                     _layer_block(w_proj, layer)]
                  + cast_in),
        out_specs=[pl.BlockSpec((tm, D), prev)] + cast_out,
        out_shape=[jax.ShapeDtypeStruct((T, D), F32)] + cast_shapes,
        scratch_shapes=[pltpu.VMEM((tm, D), F32), pltpu.VMEM((tm, D), BF16),
                        pltpu.VMEM((tm, D), F32), pltpu.VMEM((tm, D), BF16),
                        pltpu.VMEM((tm, d_ff), BF16),
                        pltpu.VMEM((n_d * MIX_A // LANE, tm, LANE), F32),
                        pltpu.VMEM((n_d, tm, LANE), F32),
                        pltpu.VMEM((tm, MIX_A), BF16)],
        compiler_params=_cparams(("arbitrary",)),
        name="mix_ffn_ple",
    )(x, *outs, *lses, ob, p, w_o, g2, w_in, w_out, gp, w_gate, w_proj, *[w for w, _ in casts])
    return res[0], res[1:]


def kernel(x, p, rel_bias, norm_ffn1, ffn1_w_in, ffn1_w_out, norm_mix, w_qkv, q_norm_a, k_norm_a, q_norm_b, k_norm_b, sink_b, w_o, norm_ffn2, ffn2_w_in, ffn2_w_out, norm_ple, w_ple_gate, w_ple_proj):
    B, S, D = x.shape
    depth = p.shape[0]
    T = B * S
    x = x.reshape(T, D)
    p = p.reshape(depth, T, p.shape[-1])

    bias_a = [_bias_tiles(rel_bias, window // (2 * d), d, 0) for window, d in DILATED_CONFIGS]
    bias_b = _bias_tiles(rel_bias, SWA_RADIUS, 1, N_HEADS)

    scale = HEAD_DIM ** -0.5 * LOG2E
    tile = lambda g, n: jnp.tile(g, (1, n))
    gains = jnp.concatenate([
        tile(q_norm_a * scale, N_HEADS), tile(k_norm_a, N_HEADS),
        jnp.ones((depth, MIX_A), F32),
        tile(q_norm_b * scale, N_HEADS), tile(k_norm_b, N_KV_B),
        jnp.ones((depth, KV_B), F32)], axis=1)[:, None]
    row = lambda g: g[:, None]
    first_half = (ffn1_w_in, ffn1_w_out, w_qkv)
    second_half = (w_o, ffn2_w_in, ffn2_w_out, w_ple_gate)
    w_first = [w[0].astype(BF16) for w in first_half]
    w_proj_b = w_ple_proj.astype(BF16)

    for i in range(depth):
        x, qkv_a, qkv_b, w_second = _ffn_qkv(
            x, row(norm_ffn1), w_first[0], w_first[1], row(norm_mix), w_first[2], gains, i,
            [(w, i) for w in second_half])
        outs, lses = [], []
        for di, (window, d) in enumerate(DILATED_CONFIGS):
            o, lse = _attn_a(qkv_a[di], bias_a[di], B, window=window, dilation=d)
            outs.append(o)
            lses.append(lse)
        ob = _attn_b(qkv_b, bias_b, sink_b[i], B)
        w_o_b, w_in2, w_out2, w_gate_b = w_second
        x, w_first = _mix_ffn_ple(
            x, outs, lses, ob, p, w_o_b, row(norm_ffn2), w_in2, w_out2, row(norm_ple), w_gate_b,
            w_proj_b, i, [(w, i + 1) for w in first_half] if i + 1 < depth else [])
    return x.reshape(B, S, D)
```

```python
import functools
import math

import jax
import jax.numpy as jnp
import numpy as np
from jax import lax
from jax.experimental import pallas as pl
from jax.experimental.pallas import tpu as pltpu

HEAD_DIM = 64
N_HEADS = 8
N_KV_B = 2
GROUP_B = N_HEADS // N_KV_B
MIX_A = N_HEADS * HEAD_DIM
KV_B = N_KV_B * HEAD_DIM
DILATED_CONFIGS = ((128, 1), (512, 4), (2048, 16))
DILATIONS = tuple(d for _, d in DILATED_CONFIGS)
SWA_RADIUS = 128
N_BUCKETS = 32
MAX_DISTANCE = 1024
EPS = 1e-6
NEG = -1e30
LOG2E = math.log2(math.e)
QBLOCK = 128
ATTN_GROUP = 16
LSE_LANES = 16
LANE = 128
BF16_TILE_ROWS = 16
FF_CHUNK = 256
ROW_TILE = 512

VMEM_LIMIT = 58 * 1024 * 1024

BF16 = jnp.bfloat16
F32 = jnp.float32


def _cparams(sem):
    return pltpu.CompilerParams(dimension_semantics=sem, vmem_limit_bytes=VMEM_LIMIT)


def _resident(shape, index_map):
    return pl.BlockSpec(shape, index_map, pipeline_mode=pl.Buffered(1))


def _layer_block(w, layer):
    return _resident((None,) + w.shape[1:], lambda i: (layer, 0, 0))


def _whole(w):
    return _resident(w.shape, lambda i: (0, 0))


def _cast_rows(rows, steps):
    r = next(r for r in range(BF16_TILE_ROWS, rows + 1, BF16_TILE_ROWS)
             if rows % r == 0 and rows // r <= steps)
    return r, rows // r


def _cast_plumbing(casts, steps):
    in_specs, out_specs, out_shapes = [], [], []
    for w, layer in casts:
        _, rows, cols = w.shape
        r, nblk = _cast_rows(rows, steps)
        in_specs.append(pl.BlockSpec((None, r, cols), functools.partial(
            lambda i, layer, nblk: (layer, jnp.minimum(i, nblk - 1), 0), layer=layer, nblk=nblk)))
        out_specs.append(pl.BlockSpec((r, cols), functools.partial(
            lambda i, nblk: (jnp.minimum(i, nblk - 1), 0), nblk=nblk)))
        out_shapes.append(jax.ShapeDtypeStruct((rows, cols), BF16))
    return in_specs, out_specs, out_shapes


def _cast_work(src_refs, dst_refs):
    def cast(src, dst):
        dst[...] = src[...].astype(BF16)
    return [functools.partial(cast, src, dst) for src, dst in zip(src_refs, dst_refs)]


def _rms_scale(x):
    return lax.rsqrt(jnp.mean(x * x, axis=-1, keepdims=True) + EPS)


def _ffn_hidden(h, win_ref, act_ref, fillers=()):
    d_ff = act_ref.shape[-1]
    chunks = range(0, d_ff, FF_CHUNK)
    fillers = list(fillers)
    per_chunk = -(-len(fillers) // len(chunks))
    for c in chunks:
        e = min(c + FF_CHUNK, d_ff)
        gate = jnp.dot(h, win_ref[:, c:e], preferred_element_type=F32)
        up = jnp.dot(h, win_ref[:, d_ff + c:d_ff + e], preferred_element_type=F32)
        act_ref[:, c:e] = (gate * jax.nn.sigmoid(gate) * up).astype(BF16)
        for fill in fillers[:per_chunk]:
            fill()
        del fillers[:per_chunk]


def _ffn_down(act_ref, wout_ref):
    return jnp.dot(act_ref[...], wout_ref[...], preferred_element_type=F32)


def _head_mean_sq(y):
    sq = y * y
    lo_half = lax.broadcasted_iota(jnp.int32, sq.shape, 1) < HEAD_DIM
    s_lo = jnp.sum(jnp.where(lo_half, sq, 0.0), axis=-1, keepdims=True)
    s_hi = jnp.sum(jnp.where(lo_half, 0.0, sq), axis=-1, keepdims=True)
    return jnp.where(lo_half, s_lo, s_hi) * (1.0 / HEAD_DIM)


def _qk_normed(y, gain):
    return y * lax.rsqrt(_head_mean_sq(y) + EPS) * gain


def _qkv_work(x_ref, g_ref, w_ref, gq_ref, a_refs, qb_ref, kb_ref, vb_ref, h_ref, scr_ref,
              cls_ref):
    n_d = len(DILATIONS)
    tm = x_ref.shape[0]
    n_blk = MIX_A // LANE

    def normalise():
        x = x_ref[...]
        h_ref[...] = (x * _rms_scale(x) * g_ref[...]).astype(BF16)

    def project_a(gi, normed):
        col = gi * MIX_A
        y = jnp.dot(h_ref[...], w_ref[:, col:col + MIX_A], preferred_element_type=F32)
        for c in range(n_blk):
            yc = y[:, c * LANE:(c + 1) * LANE]
            if normed:
                yc = _qk_normed(yc, gq_ref[:, col + c * LANE:col + (c + 1) * LANE])
            scr_ref[gi * n_blk + c] = yc

    def store_a(gi, di, d):
        out = a_refs[gi * n_d + di]
        d_prev = DILATIONS[di - 1] if di else 1
        ratio = d // d_prev
        for r in range(d):
            a, b = r % d_prev, r // d_prev
            for c in range(n_blk):
                if d == 1:
                    rows = scr_ref[gi * n_blk + c]
                else:
                    src = scr_ref.at[gi * n_blk + c] if d_prev == 1 else cls_ref.at[c]
                    rows = src[pl.ds(a * (tm // d_prev) + b, tm // d, stride=ratio), :]
                    if di + 1 < n_d:
                        cls_ref[c, r * (tm // d):(r + 1) * (tm // d), :] = rows
                out[:, r * MIX_A + c * LANE:r * MIX_A + (c + 1) * LANE] = rows.astype(BF16)

    def project_qb():
        col = 3 * MIX_A
        yq = jnp.dot(h_ref[...], w_ref[:, col:col + MIX_A], preferred_element_type=F32)
        for c in range(n_blk):
            cols = slice(c * LANE, (c + 1) * LANE)
            gain = gq_ref[:, col + c * LANE:col + (c + 1) * LANE]
            qb_ref[:, cols] = _qk_normed(yq[:, cols], gain).astype(BF16)

    def project_kvb():
        col = 4 * MIX_A
        ykv = jnp.dot(h_ref[...], w_ref[:, col:col + 2 * KV_B], preferred_element_type=F32)
        lane = lax.broadcasted_iota(jnp.int32, (tm, LANE), 1)
        for ref, t in ((kb_ref, _qk_normed(ykv[:, :KV_B], gq_ref[:, col:col + KV_B])),
                       (vb_ref, ykv[:, KV_B:])):
            swapped = pltpu.roll(t, HEAD_DIM, 1)
            ref[:, :LANE] = jnp.where(lane < HEAD_DIM, t, swapped).astype(BF16)
            ref[:, LANE:] = jnp.where(lane < HEAD_DIM, swapped, t).astype(BF16)

    work = [normalise]
    for gi, normed in enumerate((True, True, False)):
        work.append(functools.partial(project_a, gi, normed))
        work += [functools.partial(store_a, gi, di, d) for di, d in enumerate(DILATIONS)]
    return work + [project_qb, project_kvb]


def _ffn_qkv_kernel(x_ref, g1_ref, win_ref, wout_ref, gm_ref, wqkv_ref, gq_ref, *refs, n_cast):
    n_d = len(DILATIONS)
    cast_src, refs = refs[:n_cast], refs[n_cast:]
    x1_ref, qkv_a_refs, qkv_b_ref, refs = refs[0], refs[1:1 + n_d], refs[1 + n_d], refs[2 + n_d:]
    cast_dst, (x1_carry, x1_work, act_ref, hm_ref, scr_ref, cls_ref) = refs[:n_cast], refs[n_cast:]
    a_refs = [qkv_a_refs[di].at[:, gi * d * MIX_A:(gi + 1) * d * MIX_A]
              for gi in range(3) for di, d in enumerate(DILATIONS)]
    qb_ref = qkv_b_ref.at[:, :MIX_A]
    kb_ref = qkv_b_ref.at[:, MIX_A:MIX_A + N_KV_B * LANE]
    vb_ref = qkv_b_ref.at[:, MIX_A + N_KV_B * LANE:]
    i = pl.program_id(0)
    last = pl.num_programs(0) - 1
    cast_work = _cast_work(cast_src, cast_dst)

    def qkv_work(src_ref):
        return _qkv_work(src_ref, gm_ref, wqkv_ref, gq_ref, a_refs, qb_ref, kb_ref, vb_ref,
                         hm_ref, scr_ref, cls_ref)

    def stage1(fillers=()):
        x = x_ref[...]
        _ffn_hidden((x * _rms_scale(x) * g1_ref[...]).astype(BF16), win_ref, act_ref, fillers)
        x1 = x_ref[...] + 0.5 * _ffn_down(act_ref, wout_ref)
        x1_ref[...] = x1
        x1_carry[...] = x1

    @pl.when(i == 0)
    def _():
        stage1(cast_work)

    @pl.when(jnp.logical_and(i > 0, i < last))
    def _():
        x1_work[...] = x1_carry[...]
        stage1(qkv_work(x1_work) + cast_work)

    @pl.when(i == last)
    def _():
        for work in qkv_work(x1_carry) + cast_work:
            work()


def _ffn_qkv(x, g1, w_in, w_out, gm, w_qkv, gains, layer, casts, *, tm=ROW_TILE):
    T, D = x.shape
    nt = T // tm
    d_ff = w_out.shape[0]
    cast_in, cast_out, cast_shapes = _cast_plumbing(casts, nt + 1)
    cur = lambda i: (jnp.minimum(i, nt - 1), 0)
    prev = lambda i: (jnp.maximum(i - 1, 0), 0)
    gain = lambda width: _resident((None, 1, width), lambda i: (layer, 0, 0))
    b_width = MIX_A + 2 * N_KV_B * LANE
    shapes = [(T // d, 3 * d * MIX_A) for d in DILATIONS] + [(T, b_width)]
    blocks = [(tm // d, 3 * d * MIX_A) for d in DILATIONS] + [(tm, b_width)]
    outs = pl.pallas_call(
        functools.partial(_ffn_qkv_kernel, n_cast=len(casts)),
        grid=(nt + 1,),
        in_specs=[
            pl.BlockSpec((tm, D), cur), gain(D), _whole(w_in), _whole(w_out), gain(D),
            _whole(w_qkv), gain(gains.shape[-1]),
        ] + cast_in,
        out_specs=([pl.BlockSpec((tm, D), cur)] + [pl.BlockSpec(b, prev) for b in blocks]
                   + cast_out),
        out_shape=([jax.ShapeDtypeStruct((T, D), F32)]
                   + [jax.ShapeDtypeStruct(s, BF16) for s in shapes] + cast_shapes),
        scratch_shapes=[pltpu.VMEM((tm, D), F32), pltpu.VMEM((tm, D), F32),
                        pltpu.VMEM((tm, d_ff), BF16), pltpu.VMEM((tm, D), BF16),
                        pltpu.VMEM((3 * MIX_A // LANE, tm, LANE), F32),
                        pltpu.VMEM((MIX_A // LANE, tm, LANE), F32)],
        compiler_params=_cparams(("arbitrary",)),
        name="ffn_qkv",
    )(x, g1, w_in, w_out, gm, w_qkv, gains, *[w for w, _ in casts])
    n_d = len(DILATIONS)
    return outs[0], outs[1:1 + n_d], outs[1 + n_d], outs[2 + n_d:]


def _t5_bucket_np(rel):
    half = N_BUCKETS // 2
    max_exact = half // 2
    ret = np.where(rel > 0, half, 0)
    n = np.abs(rel)
    nf = np.maximum(n, 1).astype(np.float64)
    large = max_exact + (np.log(nf / max_exact) / math.log(MAX_DISTANCE / max_exact)
                         * (half - max_exact)).astype(np.int64)
    large = np.minimum(large, half - 1)
    return ret + np.where(n < max_exact, n, large)


def _bucket_tiles(radius, dilation):
    w = QBLOCK + 2 * radius
    key = np.arange(w)[:, None]
    q = np.arange(QBLOCK)[None, :]
    tiles = []
    for off in (0, -radius, -2 * radius):
        rel = off + key - q
        tiles.append(np.where(np.abs(rel) <= radius, _t5_bucket_np(rel * dilation), -1))
    return np.stack(tiles).astype(np.int32)


def _bias_kernel(rb_ref, idx_ref, o_ref, *, head0, buckets):
    idx = idx_ref[...]
    for h in range(N_HEADS):
        tile = jnp.full(idx.shape, NEG, F32)
        for b in buckets:
            tile = jnp.where(idx == b, rb_ref[b, head0 + h] * LOG2E, tile)
        o_ref[h] = tile


def _bias_tiles(rel_bias, radius, dilation, head0):
    idx_np = _bucket_tiles(radius, dilation)
    buckets = tuple(int(b) for b in np.unique(idx_np[idx_np >= 0]))
    nv, w, bq = idx_np.shape
    return pl.pallas_call(
        functools.partial(_bias_kernel, head0=head0, buckets=buckets),
        grid=(nv,),
        in_specs=[
            pl.BlockSpec(memory_space=pltpu.SMEM),
            pl.BlockSpec((None, w, bq), lambda v: (v, 0, 0)),
        ],
        out_specs=pl.BlockSpec((None, N_HEADS, w, bq), lambda v: (v, 0, 0, 0)),
        out_shape=jax.ShapeDtypeStruct((nv, N_HEADS, w, bq), F32),
        compiler_params=_cparams(("arbitrary",)),
        name="bias_tiles",
    )(rel_bias, jnp.asarray(idx_np))


def _window(i, nb, radius, length):
    w = QBLOCK + 2 * radius
    start = pl.multiple_of(jnp.clip(i * QBLOCK - radius, 0, length - w), radius)
    var = jnp.where(i == 0, 0, jnp.where(i == nb - 1, 2, 1))
    return start, var


_NT = (((1,), (1,)), ((), ()))
_TN = (((0,), (0,)), ((), ()))


def _pair_attention(qp, kp, vp, bias_lo, bias_hi, sink_lo=None, sink_hi=None):
    lane = lax.broadcasted_iota(jnp.int32, qp.shape, 1)
    zero = jnp.zeros_like(qp)
    vt = vp.T
    ones = jnp.ones((HEAD_DIM, vt.shape[1]), vt.dtype)
    outs, lses = [], []
    for hi, (bias, sink) in enumerate(((bias_lo, sink_lo), (bias_hi, sink_hi))):
        keep = (lane >= HEAD_DIM) if hi else (lane < HEAD_DIM)
        s = lax.dot_general(kp, jnp.where(keep, qp, zero), _NT, preferred_element_type=F32) + bias
        m = jnp.max(s, axis=0, keepdims=True)
        if sink is not None:
            m = jnp.maximum(m, sink)
        p = jnp.exp2(s - m).astype(BF16)
        vt_den = jnp.concatenate([ones, vt[HEAD_DIM:]] if hi else [vt[:HEAD_DIM], ones], axis=0)
        o = jnp.dot(vt_den, p, preferred_element_type=F32)
        den = o[:1] if hi else o[HEAD_DIM:HEAD_DIM + 1]
        if sink is not None:
            den = den + jnp.exp2(sink - m)
        outs.append((o[HEAD_DIM:] if hi else o[:HEAD_DIM]) * (1.0 / den))
        lses.append(m + jnp.log2(den))
    o_t = jnp.concatenate(outs, axis=0)
    return o_t.astype(BF16).T, lses[0], lses[1]


def _attn_a_kernel(q_ref, k_ref, v_ref, bias_ref, o_ref, lse_ref, *,
                   radius, length, nb, group, classes):
    w = QBLOCK + 2 * radius
    for cl, g in [(cl, g) for cl in range(classes) for g in range(group)]:
        start, var = _window(pl.program_id(2) * group + g, nb, radius, length)
        rows = slice(g * QBLOCK, (g + 1) * QBLOCK)
        lses = []
        for pr in range(N_HEADS // 2):
            cols = slice(cl * MIX_A + pr * LANE, cl * MIX_A + (pr + 1) * LANE)
            o, lse_lo, lse_hi = _pair_attention(
                q_ref[rows, cols], k_ref[pl.ds(start, w), cols], v_ref[pl.ds(start, w), cols],
                bias_ref[var, 2 * pr], bias_ref[var, 2 * pr + 1])
            o_ref[rows, cols] = o
            lses += [lse_lo, lse_hi]
        lse_t = jnp.concatenate([jnp.broadcast_to(l, (LSE_LANES, QBLOCK)) for l in lses], axis=0)
        lse_ref[rows, cl * LANE:(cl + 1) * LANE] = lse_t.T


def _attn_a(qkv, bias, batch, *, window, dilation):
    d = dilation
    L = qkv.shape[0] // batch
    nb = L // QBLOCK
    radius = window // (2 * d)
    group = min(nb, ATTN_GROUP)
    classes = min(d, ATTN_GROUP // group)
    n_cg = d // classes
    qkv = qkv.reshape(batch, L, qkv.shape[-1])
    o, lse = pl.pallas_call(
        functools.partial(_attn_a_kernel, radius=radius, length=L, nb=nb, group=group,
                          classes=classes),
        grid=(batch, n_cg, nb // group),
        in_specs=[
            pl.BlockSpec((None, group * QBLOCK, classes * MIX_A), lambda b, r, i: (b, i, r)),
            pl.BlockSpec((None, L, classes * MIX_A), lambda b, r, i: (b, 0, n_cg + r)),
            pl.BlockSpec((None, L, classes * MIX_A), lambda b, r, i: (b, 0, 2 * n_cg + r)),
            _resident(bias.shape, lambda b, r, i: (0, 0, 0, 0)),
        ],
        out_specs=[
            pl.BlockSpec((None, group * QBLOCK, classes * MIX_A), lambda b, r, i: (b, i, r)),
            pl.BlockSpec((None, group * QBLOCK, classes * LANE), lambda b, r, i: (b, i, r)),
        ],
        out_shape=[
            jax.ShapeDtypeStruct((batch, L, d * MIX_A), BF16),
            jax.ShapeDtypeStruct((batch, L, d * LANE), F32),
        ],
        compiler_params=_cparams(("parallel", "parallel", "arbitrary")),
        name=f"attn_a_d{d}",
    )(qkv, qkv, qkv, bias)
    return o.reshape(batch * L, d * MIX_A), lse.reshape(batch * L, d * LANE)


def _attn_b_kernel(sink_ref, q_ref, k_ref, v_ref, bias_ref, o_ref, *, radius, length, nb, group):
    w = QBLOCK + 2 * radius
    for g in range(group):
        start, var = _window(pl.program_id(1) * group + g, nb, radius, length)
        rows = slice(g * QBLOCK, (g + 1) * QBLOCK)
        for pr in range(N_HEADS // 2):
            cols = slice(pr * LANE, (pr + 1) * LANE)
            kv_cols = slice(2 * pr // GROUP_B * LANE, (2 * pr // GROUP_B + 1) * LANE)
            lo, hi = 2 * pr, 2 * pr + 1
            o, _, _ = _pair_attention(
                q_ref[rows, cols], k_ref[pl.ds(start, w), kv_cols], v_ref[pl.ds(start, w), kv_cols],
                bias_ref[var, lo], bias_ref[var, hi], sink_ref[lo] * LOG2E, sink_ref[hi] * LOG2E)
            o_ref[rows, cols] = o


def _attn_b(qkv, bias, sink, batch):
    S = qkv.shape[0] // batch
    nb = S // QBLOCK
    group = min(nb, ATTN_GROUP)
    kv_width = N_KV_B * LANE
    k_block = MIX_A // kv_width
    qkv = qkv.reshape(batch, S, qkv.shape[-1])
    o = pl.pallas_call(
        functools.partial(_attn_b_kernel, radius=SWA_RADIUS, length=S, nb=nb, group=group),
        grid=(batch, nb // group),
        in_specs=[
            pl.BlockSpec(memory_space=pltpu.SMEM),
            pl.BlockSpec((None, group * QBLOCK, MIX_A), lambda b, i: (b, i, 0)),
            pl.BlockSpec((None, S, kv_width), lambda b, i: (b, 0, k_block)),
            pl.BlockSpec((None, S, kv_width), lambda b, i: (b, 0, k_block + 1)),
            _resident(bias.shape, lambda b, i: (0, 0, 0, 0)),
        ],
        out_specs=pl.BlockSpec((None, group * QBLOCK, MIX_A), lambda b, i: (b, i, 0)),
        out_shape=jax.ShapeDtypeStruct((batch, S, MIX_A), BF16),
        compiler_params=_cparams(("parallel", "arbitrary")),
        name="attn_b",
    )(sink, qkv, qkv, qkv, bias)
    return o.reshape(batch * S, MIX_A)


def _merge_work(o_refs, l_refs, oscr_ref, lscr_ref, cls_ref, mrg_ref):
    n_d = len(DILATIONS)
    tm = mrg_ref.shape[0]
    n_blk = MIX_A // LANE

    def unpermute(di, d, r):
        d_prev = DILATIONS[di - 1] if di else 1
        a, b = r % d_prev, r // d_prev
        lse = l_refs[di][:, r * LANE:(r + 1) * LANE]
        outs = [o_refs[di][:, r * MIX_A + c * LANE:r * MIX_A + (c + 1) * LANE].astype(F32)
                for c in range(n_blk)]
        if d_prev == 1:
            rows = slice(None) if d == 1 else pl.ds(r, tm // d, stride=d)
            lscr_ref[di, rows, :] = lse
            for c in range(n_blk):
                oscr_ref[di * n_blk + c, rows, :] = outs[c]
        else:
            rows = pl.ds(a * (tm // d_prev) + b, tm // d, stride=d // d_prev)
            cls_ref[n_blk, rows, :] = lse
            for c in range(n_blk):
                cls_ref[c, rows, :] = outs[c]

    def regroup(di, a):
        d_prev = DILATIONS[di - 1]
        seg = slice(a * (tm // d_prev), (a + 1) * (tm // d_prev))
        rows = pl.ds(a, tm // d_prev, stride=d_prev)
        lscr_ref[di, rows, :] = cls_ref[n_blk, seg, :]
        for c in range(n_blk):
            oscr_ref[di * n_blk + c, rows, :] = cls_ref[c, seg, :]

    def weights():
        lse = [lscr_ref[di] for di in range(n_d)]
        mx = functools.reduce(jnp.maximum, lse)
        ex = [jnp.exp2(l - mx) for l in lse]
        inv = 1.0 / functools.reduce(jnp.add, ex)
        for di, e in enumerate(ex):
            lscr_ref[di] = e * inv

    def merge(c):
        lane = lax.broadcasted_iota(jnp.int32, (tm, LANE), 1)
        lo, hi = 2 * c * LSE_LANES, (2 * c + 1) * LSE_LANES
        merged = jnp.zeros((tm, LANE), F32)
        for di in range(n_d):
            wt = lscr_ref[di]
            wfull = jnp.where(lane < HEAD_DIM, wt[:, lo:lo + 1], wt[:, hi:hi + 1])
            merged = merged + wfull * oscr_ref[di * n_blk + c]
        mrg_ref[:, c * LANE:(c + 1) * LANE] = merged.astype(BF16)

    work = []
    for di, d in enumerate(DILATIONS):
        work += [functools.partial(unpermute, di, d, r) for r in range(d)]
        if di and DILATIONS[di - 1] > 1:
            work += [functools.partial(regroup, di, a) for a in range(DILATIONS[di - 1])]
    return work + [weights] + [functools.partial(merge, c) for c in range(n_blk)]


def _mix_ffn_ple_kernel(x_ref, *refs, n_cast):
    n_d = len(DILATIONS)
    o_refs, l_refs, refs = refs[:n_d], refs[n_d:2 * n_d], refs[2 * n_d:]
    (ob_ref, p_ref, wo_ref, g2_ref, win_ref, wout_ref, gp_ref, wg_ref, wp_ref), refs = (
        refs[:9], refs[9:])
    cast_src, out_ref, cast_dst, refs = (refs[:n_cast], refs[n_cast],
                                         refs[n_cast + 1:2 * n_cast + 1], refs[2 * n_cast + 1:])
    x2_carry, h_carry, x2_work, h_work, act_ref, oscr_ref, lscr_ref, cls_ref, mrg_ref = refs

    i = pl.program_id(0)
    last = pl.num_programs(0) - 1
    cast_work = _cast_work(cast_src, cast_dst)

    def merge_work():
        return _merge_work(o_refs, l_refs, oscr_ref, lscr_ref, cls_ref, mrg_ref)

    def project():
        x2 = (x_ref[...]
              + jnp.dot(mrg_ref[...], wo_ref[:MIX_A, :], preferred_element_type=F32)
              + jnp.dot(ob_ref[...], wo_ref[MIX_A:, :], preferred_element_type=F32))
        x2_carry[...] = x2
        h_carry[...] = (x2 * _rms_scale(x2) * g2_ref[...]).astype(BF16)

    def stage2(x2_ref, h_ref, fillers=(), between=lambda: None):
        _ffn_hidden(h_ref[...], win_ref, act_ref, fillers)
        x3 = x2_ref[...] + 0.5 * _ffn_down(act_ref, wout_ref)
        hp = (x3 * _rms_scale(x3) * gp_ref[...]).astype(BF16)
        between()
        gate = jax.nn.sigmoid(jnp.dot(hp, wg_ref[...], preferred_element_type=F32))
        proj = jnp.dot(p_ref[...].astype(BF16), wp_ref[...], preferred_element_type=F32)
        out_ref[...] = x3 + gate * proj

    @pl.when(i == 0)
    def _():
        for work in merge_work() + cast_work:
            work()
        project()

    @pl.when(jnp.logical_and(i > 0, i < last))
    def _():
        x2_work[...] = x2_carry[...]
        h_work[...] = h_carry[...]
        stage2(x2_work, h_work, merge_work() + cast_work, project)

    @pl.when(i == last)
    def _():
        stage2(x2_carry, h_carry, cast_work)


def _mix_ffn_ple(x, outs, lses, ob, p, w_o, g2, w_in, w_out, gp, w_gate, w_proj, layer, casts,
                 *, tm=ROW_TILE):
    T, D = x.shape
    nt = T // tm
    d_ff = w_out.shape[0]
    n_d = len(DILATIONS)
    cast_in, cast_out, cast_shapes = _cast_plumbing(casts, nt + 1)
    cur = lambda i: (jnp.minimum(i, nt - 1), 0)
    prev = lambda i: (jnp.maximum(i - 1, 0), 0)
    gain = lambda: _resident((None, 1, D), lambda i: (layer, 0, 0))
    res = pl.pallas_call(
        functools.partial(_mix_ffn_ple_kernel, n_cast=len(casts)),
        grid=(nt + 1,),
        in_specs=([pl.BlockSpec((tm, D), cur)]
                  + [pl.BlockSpec((tm // d, d * MIX_A), cur) for d in DILATIONS]
                  + [pl.BlockSpec((tm // d, d * LANE), cur) for d in DILATIONS]
                  + [pl.BlockSpec((tm, MIX_A), cur),
                     pl.BlockSpec((None, tm, p.shape[-1]),
                                  lambda i: (layer, jnp.maximum(i - 1, 0), 0)),
                     _whole(w_o), gain(), _whole(w_in), _whole(w_out), gain(), _whole(w_gate),
                     _layer_block(w_proj, layer)]
                  + cast_in),
        out_specs=[pl.BlockSpec((tm, D), prev)] + cast_out,
        out_shape=[jax.ShapeDtypeStruct((T, D), F32)] + cast_shapes,
        scratch_shapes=[pltpu.VMEM((tm, D), F32), pltpu.VMEM((tm, D), BF16),
                        pltpu.VMEM((tm, D), F32), pltpu.VMEM((tm, D), BF16),
                        pltpu.VMEM((tm, d_ff), BF16),
                        pltpu.VMEM((n_d * MIX_A // LANE, tm, LANE), F32),
                        pltpu.VMEM((n_d, tm, LANE), F32),
                        pltpu.VMEM((MIX_A // LANE + 1, tm, LANE), F32),
                        pltpu.VMEM((tm, MIX_A), BF16)],
        compiler_params=_cparams(("arbitrary",)),
        name="mix_ffn_ple",
    )(x, *outs, *lses, ob, p, w_o, g2, w_in, w_out, gp, w_gate, w_proj, *[w for w, _ in casts])
    return res[0], res[1:]


def kernel(x, p, rel_bias, norm_ffn1, ffn1_w_in, ffn1_w_out, norm_mix, w_qkv, q_norm_a, k_norm_a, q_norm_b, k_norm_b, sink_b, w_o, norm_ffn2, ffn2_w_in, ffn2_w_out, norm_ple, w_ple_gate, w_ple_proj):
    B, S, D = x.shape
    depth = p.shape[0]
    T = B * S
    x = x.reshape(T, D)
    p = p.reshape(depth, T, p.shape[-1])

    bias_a = [_bias_tiles(rel_bias, window // (2 * d), d, 0) for window, d in DILATED_CONFIGS]
    bias_b = _bias_tiles(rel_bias, SWA_RADIUS, 1, N_HEADS)

    scale = HEAD_DIM ** -0.5 * LOG2E
    tile = lambda g, n: jnp.tile(g, (1, n))
    gains = jnp.concatenate([
        tile(q_norm_a * scale, N_HEADS), tile(k_norm_a, N_HEADS),
        jnp.ones((depth, MIX_A), F32),
        tile(q_norm_b * scale, N_HEADS), tile(k_norm_b, N_KV_B),
        jnp.ones((depth, KV_B), F32)], axis=1)[:, None]
    row = lambda g: g[:, None]
    first_half = (ffn1_w_in, ffn1_w_out, w_qkv)
    second_half = (w_o, ffn2_w_in, ffn2_w_out, w_ple_gate)
    w_first = [w[0].astype(BF16) for w in first_half]
    w_proj_b = w_ple_proj.astype(BF16)

    for i in range(depth):
        x, qkv_a, qkv_b, w_second = _ffn_qkv(
            x, row(norm_ffn1), w_first[0], w_first[1], row(norm_mix), w_first[2], gains, i,
            [(w, i) for w in second_half])
        outs, lses = [], []
        for di, (window, d) in enumerate(DILATED_CONFIGS):
            o, lse = _attn_a(qkv_a[di], bias_a[di], B, window=window, dilation=d)
            outs.append(o)
            lses.append(lse)
        ob = _attn_b(qkv_b, bias_b, sink_b[i], B)
        w_o_b, w_in2, w_out2, w_gate_b = w_second
        x, w_first = _mix_ffn_ple(
            x, outs, lses, ob, p, w_o_b, row(norm_ffn2), w_in2, w_out2, row(norm_ple), w_gate_b,
            w_proj_b, i, [(w, i + 1) for w in first_half] if i + 1 < depth else [])
    return x.reshape(B, S, D)
```

```python
import functools
import math

import jax
import jax.numpy as jnp
import numpy as np
from jax import lax
from jax.experimental import pallas as pl
from jax.experimental.pallas import tpu as pltpu

HEAD_DIM = 64
N_HEADS = 8
N_KV_B = 2
GROUP_B = N_HEADS // N_KV_B
MIX_A = N_HEADS * HEAD_DIM
KV_B = N_KV_B * HEAD_DIM
DILATED_CONFIGS = ((128, 1), (512, 4), (2048, 16))
DILATIONS = tuple(d for _, d in DILATED_CONFIGS)
SWA_RADIUS = 128
N_BUCKETS = 32
MAX_DISTANCE = 1024
EPS = 1e-6
NEG = -1e30
LOG2E = math.log2(math.e)
QBLOCK = 128
ATTN_GROUP = 16
LSE_LANES = 16
LANE = 128
BF16_TILE_ROWS = 16
FF_CHUNK = 256
ROW_TILE = 512

VMEM_LIMIT = 58 * 1024 * 1024

BF16 = jnp.bfloat16
F32 = jnp.float32


def _cparams(sem):
    return pltpu.CompilerParams(dimension_semantics=sem, vmem_limit_bytes=VMEM_LIMIT)


def _resident(shape, index_map):
    return pl.BlockSpec(shape, index_map, pipeline_mode=pl.Buffered(1))


def _layer_block(w, layer):
    return _resident((None,) + w.shape[1:], lambda i: (layer, 0, 0))


def _whole(w):
    return _resident(w.shape, lambda i: (0, 0))


def _cast_rows(rows, steps):
    r = next(r for r in range(BF16_TILE_ROWS, rows + 1, BF16_TILE_ROWS)
             if rows % r == 0 and rows // r <= steps)
    return r, rows // r


def _cast_plumbing(casts, steps):
    in_specs, out_specs, out_shapes = [], [], []
    for w, layer in casts:
        _, rows, cols = w.shape
        r, nblk = _cast_rows(rows, steps)
        in_specs.append(pl.BlockSpec((None, r, cols), functools.partial(
            lambda i, layer, nblk: (layer, jnp.minimum(i, nblk - 1), 0), layer=layer, nblk=nblk)))
        out_specs.append(pl.BlockSpec((r, cols), functools.partial(
            lambda i, nblk: (jnp.minimum(i, nblk - 1), 0), nblk=nblk)))
        out_shapes.append(jax.ShapeDtypeStruct((rows, cols), BF16))
    return in_specs, out_specs, out_shapes


def _cast_work(src_refs, dst_refs):
    def cast(src, dst):
        dst[...] = src[...].astype(BF16)
    return [functools.partial(cast, src, dst) for src, dst in zip(src_refs, dst_refs)]


def _rms_scale(x):
    return lax.rsqrt(jnp.mean(x * x, axis=-1, keepdims=True) + EPS)


def _ffn_hidden(h, win_ref, act_ref, fillers=()):
    d_ff = act_ref.shape[-1]
    chunks = range(0, d_ff, FF_CHUNK)
    fillers = list(fillers)
    per_chunk = -(-len(fillers) // len(chunks))
    for c in chunks:
        e = min(c + FF_CHUNK, d_ff)
        gate = jnp.dot(h, win_ref[:, c:e], preferred_element_type=F32)
        up = jnp.dot(h, win_ref[:, d_ff + c:d_ff + e], preferred_element_type=F32)
        act_ref[:, c:e] = (gate * jax.nn.sigmoid(gate) * up).astype(BF16)
        for fill in fillers[:per_chunk]:
            fill()
        del fillers[:per_chunk]


def _ffn_down(act_ref, wout_ref):
    return jnp.dot(act_ref[...], wout_ref[...], preferred_element_type=F32)


def _head_mean_sq(y):
    sq = y * y
    lo_half = lax.broadcasted_iota(jnp.int32, sq.shape, 1) < HEAD_DIM
    s_lo = jnp.sum(jnp.where(lo_half, sq, 0.0), axis=-1, keepdims=True)
    s_hi = jnp.sum(jnp.where(lo_half, 0.0, sq), axis=-1, keepdims=True)
    return jnp.where(lo_half, s_lo, s_hi) * (1.0 / HEAD_DIM)


def _qk_normed(y, gain):
    return y * lax.rsqrt(_head_mean_sq(y) + EPS) * gain


def _qkv_work(x_ref, g_ref, w_ref, gq_ref, a_refs, qb_ref, kb_ref, vb_ref, h_ref, scr_ref,
              cls_ref):
    n_d = len(DILATIONS)
    tm = x_ref.shape[0]
    n_blk = MIX_A // LANE

    def normalise():
        x = x_ref[...]
        h_ref[...] = (x * _rms_scale(x) * g_ref[...]).astype(BF16)

    def project_a(gi, normed):
        col = gi * MIX_A
        y = jnp.dot(h_ref[...], w_ref[:, col:col + MIX_A], preferred_element_type=F32)
        for c in range(n_blk):
            yc = y[:, c * LANE:(c + 1) * LANE]
            if normed:
                yc = _qk_normed(yc, gq_ref[:, col + c * LANE:col + (c + 1) * LANE])
            scr_ref[gi * n_blk + c] = yc

    def store_a(gi, di, d):
        out = a_refs[gi * n_d + di]
        d_prev = DILATIONS[di - 1] if di else 1
        ratio = d // d_prev
        for r in range(d):
            a, b = r % d_prev, r // d_prev
            for c in range(n_blk):
                if d == 1:
                    rows = scr_ref[gi * n_blk + c]
                else:
                    src = scr_ref.at[gi * n_blk + c] if d_prev == 1 else cls_ref.at[c]
                    rows = src[pl.ds(a * (tm // d_prev) + b, tm // d, stride=ratio), :]
                    if di + 1 < n_d:
                        cls_ref[c, r * (tm // d):(r + 1) * (tm // d), :] = rows
                out[:, r * MIX_A + c * LANE:r * MIX_A + (c + 1) * LANE] = rows.astype(BF16)

    def project_qb():
        col = 3 * MIX_A
        yq = jnp.dot(h_ref[...], w_ref[:, col:col + MIX_A], preferred_element_type=F32)
        for c in range(n_blk):
            cols = slice(c * LANE, (c + 1) * LANE)
            gain = gq_ref[:, col + c * LANE:col + (c + 1) * LANE]
            qb_ref[:, cols] = _qk_normed(yq[:, cols], gain).astype(BF16)

    def project_kvb():
        col = 4 * MIX_A
        ykv = jnp.dot(h_ref[...], w_ref[:, col:col + 2 * KV_B], preferred_element_type=F32)
        lane = lax.broadcasted_iota(jnp.int32, (tm, LANE), 1)
        for ref, t in ((kb_ref, _qk_normed(ykv[:, :KV_B], gq_ref[:, col:col + KV_B])),
                       (vb_ref, ykv[:, KV_B:])):
            swapped = pltpu.roll(t, HEAD_DIM, 1)
            ref[:, :LANE] = jnp.where(lane < HEAD_DIM, t, swapped).astype(BF16)
            ref[:, LANE:] = jnp.where(lane < HEAD_DIM, swapped, t).astype(BF16)

    work = [normalise]
    for gi, normed in enumerate((True, True, False)):
        work.append(functools.partial(project_a, gi, normed))
        work += [functools.partial(store_a, gi, di, d) for di, d in enumerate(DILATIONS)]
    return work + [project_qb, project_kvb]


def _ffn_qkv_kernel(x_ref, g1_ref, win_ref, wout_ref, gm_ref, wqkv_ref, gq_ref, *refs, n_cast):
    n_d = len(DILATIONS)
    cast_src, refs = refs[:n_cast], refs[n_cast:]
    x1_ref, qkv_a_refs, qkv_b_ref, refs = refs[0], refs[1:1 + n_d], refs[1 + n_d], refs[2 + n_d:]
    cast_dst, (x1_carry, x1_work, act_ref, hm_ref, scr_ref, cls_ref) = refs[:n_cast], refs[n_cast:]
    a_refs = [qkv_a_refs[di].at[:, gi * d * MIX_A:(gi + 1) * d * MIX_A]
              for gi in range(3) for di, d in enumerate(DILATIONS)]
    qb_ref = qkv_b_ref.at[:, :MIX_A]
    kb_ref = qkv_b_ref.at[:, MIX_A:MIX_A + N_KV_B * LANE]
    vb_ref = qkv_b_ref.at[:, MIX_A + N_KV_B * LANE:]
    i = pl.program_id(0)
    last = pl.num_programs(0) - 1
    cast_work = _cast_work(cast_src, cast_dst)

    def qkv_work(src_ref):
        return _qkv_work(src_ref, gm_ref, wqkv_ref, gq_ref, a_refs, qb_ref, kb_ref, vb_ref,
                         hm_ref, scr_ref, cls_ref)

    def stage1(fillers=()):
        x = x_ref[...]
        _ffn_hidden((x * _rms_scale(x) * g1_ref[...]).astype(BF16), win_ref, act_ref, fillers)
        x1 = x_ref[...] + 0.5 * _ffn_down(act_ref, wout_ref)
        x1_ref[...] = x1
        x1_carry[...] = x1

    @pl.when(i == 0)
    def _():
        stage1(cast_work)

    @pl.when(jnp.logical_and(i > 0, i < last))
    def _():
        x1_work[...] = x1_carry[...]
        stage1(qkv_work(x1_work) + cast_work)

    @pl.when(i == last)
    def _():
        for work in qkv_work(x1_carry) + cast_work:
            work()


def _ffn_qkv(x, g1, w_in, w_out, gm, w_qkv, gains, layer, casts, *, tm=ROW_TILE):
    T, D = x.shape
    nt = T // tm
    d_ff = w_out.shape[0]
    cast_in, cast_out, cast_shapes = _cast_plumbing(casts, nt + 1)
    cur = lambda i: (jnp.minimum(i, nt - 1), 0)
    prev = lambda i: (jnp.maximum(i - 1, 0), 0)
    gain = lambda width: _resident((None, 1, width), lambda i: (layer, 0, 0))
    b_width = MIX_A + 2 * N_KV_B * LANE
    shapes = [(T // d, 3 * d * MIX_A) for d in DILATIONS] + [(T, b_width)]
    blocks = [(tm // d, 3 * d * MIX_A) for d in DILATIONS] + [(tm, b_width)]
    outs = pl.pallas_call(
        functools.partial(_ffn_qkv_kernel, n_cast=len(casts)),
        grid=(nt + 1,),
        in_specs=[
            pl.BlockSpec((tm, D), cur), gain(D), _whole(w_in), _whole(w_out), gain(D),
            _whole(w_qkv), gain(gains.shape[-1]),
        ] + cast_in,
        out_specs=([pl.BlockSpec((tm, D), cur)] + [pl.BlockSpec(b, prev) for b in blocks]
                   + cast_out),
        out_shape=([jax.ShapeDtypeStruct((T, D), F32)]
                   + [jax.ShapeDtypeStruct(s, BF16) for s in shapes] + cast_shapes),
        scratch_shapes=[pltpu.VMEM((tm, D), F32), pltpu.VMEM((tm, D), F32),
                        pltpu.VMEM((tm, d_ff), BF16), pltpu.VMEM((tm, D), BF16),
                        pltpu.VMEM((3 * MIX_A // LANE, tm, LANE), F32),
                        pltpu.VMEM((MIX_A // LANE, tm, LANE), F32)],
        compiler_params=_cparams(("arbitrary",)),
        name="ffn_qkv",
    )(x, g1, w_in, w_out, gm, w_qkv, gains, *[w for w, _ in casts])
    n_d = len(DILATIONS)
    return outs[0], outs[1:1 + n_d], outs[1 + n_d], outs[2 + n_d:]


def _t5_bucket_np(rel):
    half = N_BUCKETS // 2
    max_exact = half // 2
    ret = np.where(rel > 0, half, 0)
    n = np.abs(rel)
    nf = np.maximum(n, 1).astype(np.float64)
    large = max_exact + (np.log(nf / max_exact) / math.log(MAX_DISTANCE / max_exact)
                         * (half - max_exact)).astype(np.int64)
    large = np.minimum(large, half - 1)
    return ret + np.where(n < max_exact, n, large)


def _bucket_tiles(radius, dilation):
    w = QBLOCK + 2 * radius
    key = np.arange(w)[:, None]
    q = np.arange(QBLOCK)[None, :]
    tiles = []
    for off in (0, -radius, -2 * radius):
        rel = off + key - q
        tiles.append(np.where(np.abs(rel) <= radius, _t5_bucket_np(rel * dilation), -1))
    return np.stack(tiles).astype(np.int32)


def _bias_kernel(rb_ref, idx_ref, o_ref, *, head0, buckets):
    idx = idx_ref[...]
    for h in range(N_HEADS):
        tile = jnp.full(idx.shape, NEG, F32)
        for b in buckets:
            tile = jnp.where(idx == b, rb_ref[b, head0 + h] * LOG2E, tile)
        o_ref[h] = tile


def _bias_tiles(rel_bias, radius, dilation, head0):
    idx_np = _bucket_tiles(radius, dilation)
    buckets = tuple(int(b) for b in np.unique(idx_np[idx_np >= 0]))
    nv, w, bq = idx_np.shape
    return pl.pallas_call(
        functools.partial(_bias_kernel, head0=head0, buckets=buckets),
        grid=(nv,),
        in_specs=[
            pl.BlockSpec(memory_space=pltpu.SMEM),
            pl.BlockSpec((None, w, bq), lambda v: (v, 0, 0)),
        ],
        out_specs=pl.BlockSpec((None, N_HEADS, w, bq), lambda v: (v, 0, 0, 0)),
        out_shape=jax.ShapeDtypeStruct((nv, N_HEADS, w, bq), F32),
        compiler_params=_cparams(("arbitrary",)),
        name="bias_tiles",
    )(rel_bias, jnp.asarray(idx_np))


def _window(i, nb, radius, length):
    w = QBLOCK + 2 * radius
    start = pl.multiple_of(jnp.clip(i * QBLOCK - radius, 0, length - w), radius)
    var = jnp.where(i == 0, 0, jnp.where(i == nb - 1, 2, 1))
    return start, var


_NT = (((1,), (1,)), ((), ()))


def _pair_attention(qp, kp, vp, bias_lo, bias_hi, sink_lo=None, sink_hi=None):
    lane = lax.broadcasted_iota(jnp.int32, qp.shape, 1)
    zero = jnp.zeros_like(qp)
    vt = vp.T
    ones = jnp.ones((HEAD_DIM, vt.shape[1]), vt.dtype)
    outs, lses = [], []
    for hi, (bias, sink) in enumerate(((bias_lo, sink_lo), (bias_hi, sink_hi))):
        keep = (lane >= HEAD_DIM) if hi else (lane < HEAD_DIM)
        s = lax.dot_general(kp, jnp.where(keep, qp, zero), _NT, preferred_element_type=F32) + bias
        m = jnp.max(s, axis=0, keepdims=True)
        if sink is not None:
            m = jnp.maximum(m, sink)
        p = jnp.exp2(s - m).astype(BF16)
        vt_den = jnp.concatenate([ones, vt[HEAD_DIM:]] if hi else [vt[:HEAD_DIM], ones], axis=0)
        o = jnp.dot(vt_den, p, preferred_element_type=F32)
        den = o[:1] if hi else o[HEAD_DIM:HEAD_DIM + 1]
        if sink is not None:
            den = den + jnp.exp2(sink - m)
        outs.append((o[HEAD_DIM:] if hi else o[:HEAD_DIM]) * (1.0 / den))
        lses.append(m + jnp.log2(den))
    o_t = jnp.concatenate(outs, axis=0)
    return o_t.astype(BF16).T, lses[0], lses[1]


def _attn_a_kernel(q_ref, k_ref, v_ref, bias_ref, o_ref, lse_ref, *,
                   radius, length, nb, group, classes):
    w = QBLOCK + 2 * radius
    for cl, g in [(cl, g) for cl in range(classes) for g in range(group)]:
        start, var = _window(pl.program_id(2) * group + g, nb, radius, length)
        rows = slice(g * QBLOCK, (g + 1) * QBLOCK)
        lses = []
        for pr in range(N_HEADS // 2):
            cols = slice(cl * MIX_A + pr * LANE, cl * MIX_A + (pr + 1) * LANE)
            o, lse_lo, lse_hi = _pair_attention(
                q_ref[rows, cols], k_ref[pl.ds(start, w), cols], v_ref[pl.ds(start, w), cols],
                bias_ref[var, 2 * pr], bias_ref[var, 2 * pr + 1])
            o_ref[rows, cols] = o
            lses += [lse_lo, lse_hi]
        lse_t = jnp.concatenate([jnp.broadcast_to(l, (LSE_LANES, QBLOCK)) for l in lses], axis=0)
        lse_ref[rows, cl * LANE:(cl + 1) * LANE] = lse_t.T


def _attn_a(qkv, bias, batch, *, window, dilation):
    d = dilation
    L = qkv.shape[0] // batch
    nb = L // QBLOCK
    radius = window // (2 * d)
    group = min(nb, ATTN_GROUP)
    classes = min(d, ATTN_GROUP // group)
    n_cg = d // classes
    qkv = qkv.reshape(batch, L, qkv.shape[-1])
    o, lse = pl.pallas_call(
        functools.partial(_attn_a_kernel, radius=radius, length=L, nb=nb, group=group,
                          classes=classes),
        grid=(batch, n_cg, nb // group),
        in_specs=[
            pl.BlockSpec((None, group * QBLOCK, classes * MIX_A), lambda b, r, i: (b, i, r)),
            pl.BlockSpec((None, L, classes * MIX_A), lambda b, r, i: (b, 0, n_cg + r)),
            pl.BlockSpec((None, L, classes * MIX_A), lambda b, r, i: (b, 0, 2 * n_cg + r)),
            _resident(bias.shape, lambda b, r, i: (0, 0, 0, 0)),
        ],
        out_specs=[
            pl.BlockSpec((None, group * QBLOCK, classes * MIX_A), lambda b, r, i: (b, i, r)),
            pl.BlockSpec((None, group * QBLOCK, classes * LANE), lambda b, r, i: (b, i, r)),
        ],
        out_shape=[
            jax.ShapeDtypeStruct((batch, L, d * MIX_A), BF16),
            jax.ShapeDtypeStruct((batch, L, d * LANE), F32),
        ],
        compiler_params=_cparams(("parallel", "parallel", "arbitrary")),
        name=f"attn_a_d{d}",
    )(qkv, qkv, qkv, bias)
    return o.reshape(batch * L, d * MIX_A), lse.reshape(batch * L, d * LANE)


def _attn_b_kernel(sink_ref, q_ref, k_ref, v_ref, bias_ref, o_ref, *, radius, length, nb, group):
    w = QBLOCK + 2 * radius
    for g in range(group):
        start, var = _window(pl.program_id(1) * group + g, nb, radius, length)
        rows = slice(g * QBLOCK, (g + 1) * QBLOCK)
        for pr in range(N_HEADS // 2):
            cols = slice(pr * LANE, (pr + 1) * LANE)
            kv_cols = slice(2 * pr // GROUP_B * LANE, (2 * pr // GROUP_B + 1) * LANE)
            lo, hi = 2 * pr, 2 * pr + 1
            o, _, _ = _pair_attention(
                q_ref[rows, cols], k_ref[pl.ds(start, w), kv_cols], v_ref[pl.ds(start, w), kv_cols],
                bias_ref[var, lo], bias_ref[var, hi], sink_ref[lo] * LOG2E, sink_ref[hi] * LOG2E)
            o_ref[rows, cols] = o


def _attn_b(qkv, bias, sink, batch):
    S = qkv.shape[0] // batch
    nb = S // QBLOCK
    group = min(nb, ATTN_GROUP)
    kv_width = N_KV_B * LANE
    k_block = MIX_A // kv_width
    qkv = qkv.reshape(batch, S, qkv.shape[-1])
    o = pl.pallas_call(
        functools.partial(_attn_b_kernel, radius=SWA_RADIUS, length=S, nb=nb, group=group),
        grid=(batch, nb // group),
        in_specs=[
            pl.BlockSpec(memory_space=pltpu.SMEM),
            pl.BlockSpec((None, group * QBLOCK, MIX_A), lambda b, i: (b, i, 0)),
            pl.BlockSpec((None, S, kv_width), lambda b, i: (b, 0, k_block)),
            pl.BlockSpec((None, S, kv_width), lambda b, i: (b, 0, k_block + 1)),
            _resident(bias.shape, lambda b, i: (0, 0, 0, 0)),
        ],
        out_specs=pl.BlockSpec((None, group * QBLOCK, MIX_A), lambda b, i: (b, i, 0)),
        out_shape=jax.ShapeDtypeStruct((batch, S, MIX_A), BF16),
        compiler_params=_cparams(("parallel", "arbitrary")),
        name="attn_b",
    )(sink, qkv, qkv, qkv, bias)
    return o.reshape(batch * S, MIX_A)


def _merge_work(o_refs, l_refs, oscr_ref, lscr_ref, cls_ref, mrg_ref):
    n_d = len(DILATIONS)
    tm = mrg_ref.shape[0]
    n_blk = MIX_A // LANE

    def unpermute(di, d, r):
        d_prev = DILATIONS[di - 1] if di else 1
        a, b = r % d_prev, r // d_prev
        lse = l_refs[di][:, r * LANE:(r + 1) * LANE]
        outs = [o_refs[di][:, r * MIX_A + c * LANE:r * MIX_A + (c + 1) * LANE].astype(F32)
                for c in range(n_blk)]
        if d_prev == 1:
            rows = slice(None) if d == 1 else pl.ds(r, tm // d, stride=d)
            lscr_ref[di, rows, :] = lse
            for c in range(n_blk):
                oscr_ref[di * n_blk + c, rows, :] = outs[c]
        else:
            rows = pl.ds(a * (tm // d_prev) + b, tm // d, stride=d // d_prev)
            cls_ref[n_blk, rows, :] = lse
            for c in range(n_blk):
                cls_ref[c, rows, :] = outs[c]

    def regroup(di, a):
        d_prev = DILATIONS[di - 1]
        seg = slice(a * (tm // d_prev), (a + 1) * (tm // d_prev))
        rows = pl.ds(a, tm // d_prev, stride=d_prev)
        lscr_ref[di, rows, :] = cls_ref[n_blk, seg, :]
        for c in range(n_blk):
            oscr_ref[di * n_blk + c, rows, :] = cls_ref[c, seg, :]

    def weights():
        lse = [lscr_ref[di] for di in range(n_d)]
        mx = functools.reduce(jnp.maximum, lse)
        ex = [jnp.exp2(l - mx) for l in lse]
        inv = 1.0 / functools.reduce(jnp.add, ex)
        for di, e in enumerate(ex):
            lscr_ref[di] = e * inv

    def merge(c):
        lane = lax.broadcasted_iota(jnp.int32, (tm, LANE), 1)
        lo, hi = 2 * c * LSE_LANES, (2 * c + 1) * LSE_LANES
        merged = jnp.zeros((tm, LANE), F32)
        for di in range(n_d):
            wt = lscr_ref[di]
            wfull = jnp.where(lane < HEAD_DIM, wt[:, lo:lo + 1], wt[:, hi:hi + 1])
            merged = merged + wfull * oscr_ref[di * n_blk + c]
        mrg_ref[:, c * LANE:(c + 1) * LANE] = merged.astype(BF16)

    work = []
    for di, d in enumerate(DILATIONS):
        work += [functools.partial(unpermute, di, d, r) for r in range(d)]
        if di and DILATIONS[di - 1] > 1:
            work += [functools.partial(regroup, di, a) for a in range(DILATIONS[di - 1])]
    return work + [weights] + [functools.partial(merge, c) for c in range(n_blk)]


def _mix_ffn_ple_kernel(x_ref, *refs, n_cast):
    n_d = len(DILATIONS)
    o_refs, l_refs, refs = refs[:n_d], refs[n_d:2 * n_d], refs[2 * n_d:]
    (ob_ref, p_ref, wo_ref, g2_ref, win_ref, wout_ref, gp_ref, wg_ref, wp_ref), refs = (
        refs[:9], refs[9:])
    cast_src, out_ref, cast_dst, refs = (refs[:n_cast], refs[n_cast],
                                         refs[n_cast + 1:2 * n_cast + 1], refs[2 * n_cast + 1:])
    x2_carry, h_carry, x2_work, h_work, act_ref, oscr_ref, lscr_ref, cls_ref, mrg_ref = refs

    i = pl.program_id(0)
    last = pl.num_programs(0) - 1
    cast_work = _cast_work(cast_src, cast_dst)

    def merge_work():
        return _merge_work(o_refs, l_refs, oscr_ref, lscr_ref, cls_ref, mrg_ref)

    def project():
        x2 = (x_ref[...]
              + jnp.dot(mrg_ref[...], wo_ref[:MIX_A, :], preferred_element_type=F32)
              + jnp.dot(ob_ref[...], wo_ref[MIX_A:, :], preferred_element_type=F32))
        x2_carry[...] = x2
        h_carry[...] = (x2 * _rms_scale(x2) * g2_ref[...]).astype(BF16)

    def stage2(x2_ref, h_ref, fillers=(), between=lambda: None):
        _ffn_hidden(h_ref[...], win_ref, act_ref, fillers)
        x3 = x2_ref[...] + 0.5 * _ffn_down(act_ref, wout_ref)
        hp = (x3 * _rms_scale(x3) * gp_ref[...]).astype(BF16)
        between()
        gate = jax.nn.sigmoid(jnp.dot(hp, wg_ref[...], preferred_element_type=F32))
        proj = jnp.dot(p_ref[...].astype(BF16), wp_ref[...], preferred_element_type=F32)
        out_ref[...] = x3 + gate * proj

    @pl.when(i == 0)
    def _():
        for work in merge_work() + cast_work:
            work()
        project()

    @pl.when(jnp.logical_and(i > 0, i < last))
    def _():
        x2_work[...] = x2_carry[...]
        h_work[...] = h_carry[...]
        stage2(x2_work, h_work, merge_work() + cast_work, project)

    @pl.when(i == last)
    def _():
        stage2(x2_carry, h_carry, cast_work)


def _mix_ffn_ple(x, outs, lses, ob, p, w_o, g2, w_in, w_out, gp, w_gate, w_proj, layer, casts,
                 *, tm=ROW_TILE):
    T, D = x.shape
    nt = T // tm
    d_ff = w_out.shape[0]
    n_d = len(DILATIONS)
    cast_in, cast_out, cast_shapes = _cast_plumbing(casts, nt + 1)
    cur = lambda i: (jnp.minimum(i, nt - 1), 0)
    prev = lambda i: (jnp.maximum(i - 1, 0), 0)
    gain = lambda: _resident((None, 1, D), lambda i: (layer, 0, 0))
    res = pl.pallas_call(
        functools.partial(_mix_ffn_ple_kernel, n_cast=len(casts)),
        grid=(nt + 1,),
        in_specs=([pl.BlockSpec((tm, D), cur)]
                  + [pl.BlockSpec((tm // d, d * MIX_A), cur) for d in DILATIONS]
                  + [pl.BlockSpec((tm // d, d * LANE), cur) for d in DILATIONS]
                  + [pl.BlockSpec((tm, MIX_A), cur),
                     pl.BlockSpec((None, tm, p.shape[-1]),
                                  lambda i: (layer, jnp.maximum(i - 1, 0), 0)),
                     _whole(w_o), gain(), _whole(w_in), _whole(w_out), gain(), _whole(w_gate),
                     _layer_block(w_proj, layer)]
                  + cast_in),
        out_specs=[pl.BlockSpec((tm, D), prev)] + cast_out,
        out_shape=[jax.ShapeDtypeStruct((T, D), F32)] + cast_shapes,
        scratch_shapes=[pltpu.VMEM((tm, D), F32), pltpu.VMEM((tm, D), BF16),
                        pltpu.VMEM((tm, D), F32), pltpu.VMEM((tm, D), BF16),
                        pltpu.VMEM((tm, d_ff), BF16),
                        pltpu.VMEM((n_d * MIX_A // LANE, tm, LANE), F32),
                        pltpu.VMEM((n_d, tm, LANE), F32),
                        pltpu.VMEM((MIX_A // LANE + 1, tm, LANE), F32),
                        pltpu.VMEM((tm, MIX_A), BF16)],
        compiler_params=_cparams(("arbitrary",)),
        name="mix_ffn_ple",
    )(x, *outs, *lses, ob, p, w_o, g2, w_in, w_out, gp, w_gate, w_proj, *[w for w, _ in casts])
    return res[0], res[1:]


def kernel(x, p, rel_bias, norm_ffn1, ffn1_w_in, ffn1_w_out, norm_mix, w_qkv, q_norm_a, k_norm_a, q_norm_b, k_norm_b, sink_b, w_o, norm_ffn2, ffn2_w_in, ffn2_w_out, norm_ple, w_ple_gate, w_ple_proj):
    B, S, D = x.shape
    depth = p.shape[0]
    T = B * S
    x = x.reshape(T, D)
    p = p.reshape(depth, T, p.shape[-1])

    bias_a = [_bias_tiles(rel_bias, window // (2 * d), d, 0) for window, d in DILATED_CONFIGS]
    bias_b = _bias_tiles(rel_bias, SWA_RADIUS, 1, N_HEADS)

    scale = HEAD_DIM ** -0.5 * LOG2E
    tile = lambda g, n: jnp.tile(g, (1, n))
    gains = jnp.concatenate([
        tile(q_norm_a * scale, N_HEADS), tile(k_norm_a, N_HEADS),
        jnp.ones((depth, MIX_A), F32),
        tile(q_norm_b * scale, N_HEADS), tile(k_norm_b, N_KV_B),
        jnp.ones((depth, KV_B), F32)], axis=1)[:, None]
    row = lambda g: g[:, None]
    first_half = (ffn1_w_in, ffn1_w_out, w_qkv)
    second_half = (w_o, ffn2_w_in, ffn2_w_out, w_ple_gate)
    w_first = [w[0].astype(BF16) for w in first_half]
    w_proj_b = w_ple_proj.astype(BF16)

    for i in range(depth):
        x, qkv_a, qkv_b, w_second = _ffn_qkv(
            x, row(norm_ffn1), w_first[0], w_first[1], row(norm_mix), w_first[2], gains, i,
            [(w, i) for w in second_half])
        outs, lses = [], []
        for di, (window, d) in enumerate(DILATED_CONFIGS):
            o, lse = _attn_a(qkv_a[di], bias_a[di], B, window=window, dilation=d)
            outs.append(o)
            lses.append(lse)
        ob = _attn_b(qkv_b, bias_b, sink_b[i], B)
        w_o_b, w_in2, w_out2, w_gate_b = w_second
        x, w_first = _mix_ffn_ple(
            x, outs, lses, ob, p, w_o_b, row(norm_ffn2), w_in2, w_out2, row(norm_ple), w_gate_b,
            w_proj_b, i, [(w, i + 1) for w in first_half] if i + 1 < depth else [])
    return x.reshape(B, S, D)
```

```python
import functools
import math

import jax
import jax.numpy as jnp
import numpy as np
from jax import lax
from jax.experimental import pallas as pl
from jax.experimental.pallas import tpu as pltpu

HEAD_DIM = 64
N_HEADS = 8
N_KV_B = 2
GROUP_B = N_HEADS // N_KV_B
MIX_A = N_HEADS * HEAD_DIM
KV_B = N_KV_B * HEAD_DIM
DILATED_CONFIGS = ((128, 1), (512, 4), (2048, 16))
DILATIONS = tuple(d for _, d in DILATED_CONFIGS)
SWA_RADIUS = 128
N_BUCKETS = 32
MAX_DISTANCE = 1024
EPS = 1e-6
NEG = -1e30
LOG2E = math.log2(math.e)
QBLOCK = 128
ATTN_GROUP = 32
LSE_LANES = 16
LANE = 128
BF16_TILE_ROWS = 16
FF_CHUNK = 256
ROW_TILE = 512

VMEM_LIMIT = 58 * 1024 * 1024

BF16 = jnp.bfloat16
F32 = jnp.float32


def _cparams(sem):
    return pltpu.CompilerParams(dimension_semantics=sem, vmem_limit_bytes=VMEM_LIMIT)


def _resident(shape, index_map):
    return pl.BlockSpec(shape, index_map, pipeline_mode=pl.Buffered(1))


def _layer_block(w, layer):
    return _resident((None,) + w.shape[1:], lambda i: (layer, 0, 0))


def _whole(w):
    return _resident(w.shape, lambda i: (0, 0))


def _cast_rows(rows, steps):
    r = next(r for r in range(BF16_TILE_ROWS, rows + 1, BF16_TILE_ROWS)
             if rows % r == 0 and rows // r <= steps)
    return r, rows // r


def _cast_plumbing(casts, steps):
    in_specs, out_specs, out_shapes = [], [], []
    for w, layer in casts:
        _, rows, cols = w.shape
        r, nblk = _cast_rows(rows, steps)
        in_specs.append(pl.BlockSpec((None, r, cols), functools.partial(
            lambda i, layer, nblk: (layer, jnp.minimum(i, nblk - 1), 0), layer=layer, nblk=nblk)))
        out_specs.append(pl.BlockSpec((r, cols), functools.partial(
            lambda i, nblk: (jnp.minimum(i, nblk - 1), 0), nblk=nblk)))
        out_shapes.append(jax.ShapeDtypeStruct((rows, cols), BF16))
    return in_specs, out_specs, out_shapes


def _cast_work(src_refs, dst_refs):
    def cast(src, dst):
        dst[...] = src[...].astype(BF16)
    return [functools.partial(cast, src, dst) for src, dst in zip(src_refs, dst_refs)]


def _rms_scale(x):
    return lax.rsqrt(jnp.mean(x * x, axis=-1, keepdims=True) + EPS)


def _ffn_hidden(h, win_ref, act_ref, fillers=()):
    d_ff = act_ref.shape[-1]
    chunks = range(0, d_ff, FF_CHUNK)
    fillers = list(fillers)
    per_chunk = -(-len(fillers) // len(chunks))
    for c in chunks:
        e = min(c + FF_CHUNK, d_ff)
        gate = jnp.dot(h, win_ref[:, c:e], preferred_element_type=F32)
        up = jnp.dot(h, win_ref[:, d_ff + c:d_ff + e], preferred_element_type=F32)
        act_ref[:, c:e] = (gate * jax.nn.sigmoid(gate) * up).astype(BF16)
        for fill in fillers[:per_chunk]:
            fill()
        del fillers[:per_chunk]


def _ffn_down(act_ref, wout_ref):
    return jnp.dot(act_ref[...], wout_ref[...], preferred_element_type=F32)


def _head_mean_sq(y):
    sq = y * y
    lo_half = lax.broadcasted_iota(jnp.int32, sq.shape, 1) < HEAD_DIM
    s_lo = jnp.sum(jnp.where(lo_half, sq, 0.0), axis=-1, keepdims=True)
    s_hi = jnp.sum(jnp.where(lo_half, 0.0, sq), axis=-1, keepdims=True)
    return jnp.where(lo_half, s_lo, s_hi) * (1.0 / HEAD_DIM)


def _qk_normed(y, gain):
    return y * lax.rsqrt(_head_mean_sq(y) + EPS) * gain


def _qkv_work(x_ref, g_ref, w_ref, gq_ref, a_refs, qb_ref, kb_ref, vb_ref, h_ref, scr_ref,
              cls_ref):
    n_d = len(DILATIONS)
    tm = x_ref.shape[0]
    n_blk = MIX_A // LANE

    def normalise():
        x = x_ref[...]
        h_ref[...] = (x * _rms_scale(x) * g_ref[...]).astype(BF16)

    def project_a(gi, normed):
        col = gi * MIX_A
        y = jnp.dot(h_ref[...], w_ref[:, col:col + MIX_A], preferred_element_type=F32)
        for c in range(n_blk):
            yc = y[:, c * LANE:(c + 1) * LANE]
            if normed:
                yc = _qk_normed(yc, gq_ref[:, col + c * LANE:col + (c + 1) * LANE])
            scr_ref[gi * n_blk + c] = yc

    def store_a(gi, di, d):
        out = a_refs[gi * n_d + di]
        d_prev = DILATIONS[di - 1] if di else 1
        ratio = d // d_prev
        for r in range(d):
            a, b = r % d_prev, r // d_prev
            for c in range(n_blk):
                if d == 1:
                    rows = scr_ref[gi * n_blk + c]
                else:
                    src = scr_ref.at[gi * n_blk + c] if d_prev == 1 else cls_ref.at[c]
                    rows = src[pl.ds(a * (tm // d_prev) + b, tm // d, stride=ratio), :]
                    if di + 1 < n_d:
                        cls_ref[c, r * (tm // d):(r + 1) * (tm // d), :] = rows
                out[:, r * MIX_A + c * LANE:r * MIX_A + (c + 1) * LANE] = rows.astype(BF16)

    def project_qb():
        col = 3 * MIX_A
        yq = jnp.dot(h_ref[...], w_ref[:, col:col + MIX_A], preferred_element_type=F32)
        for c in range(n_blk):
            cols = slice(c * LANE, (c + 1) * LANE)
            gain = gq_ref[:, col + c * LANE:col + (c + 1) * LANE]
            qb_ref[:, cols] = _qk_normed(yq[:, cols], gain).astype(BF16)

    def project_kvb():
        col = 4 * MIX_A
        ykv = jnp.dot(h_ref[...], w_ref[:, col:col + 2 * KV_B], preferred_element_type=F32)
        lane = lax.broadcasted_iota(jnp.int32, (tm, LANE), 1)
        for ref, t in ((kb_ref, _qk_normed(ykv[:, :KV_B], gq_ref[:, col:col + KV_B])),
                       (vb_ref, ykv[:, KV_B:])):
            swapped = pltpu.roll(t, HEAD_DIM, 1)
            ref[:, :LANE] = jnp.where(lane < HEAD_DIM, t, swapped).astype(BF16)
            ref[:, LANE:] = jnp.where(lane < HEAD_DIM, swapped, t).astype(BF16)

    work = [normalise]
    for gi, normed in enumerate((True, True, False)):
        work.append(functools.partial(project_a, gi, normed))
        work += [functools.partial(store_a, gi, di, d) for di, d in enumerate(DILATIONS)]
    return work + [project_qb, project_kvb]


def _ffn_qkv_kernel(x_ref, g1_ref, win_ref, wout_ref, gm_ref, wqkv_ref, gq_ref, *refs, n_cast):
    n_d = len(DILATIONS)
    cast_src, refs = refs[:n_cast], refs[n_cast:]
    x1_ref, qkv_a_refs, qkv_b_ref, refs = refs[0], refs[1:1 + n_d], refs[1 + n_d], refs[2 + n_d:]
    cast_dst, (x1_carry, x1_work, act_ref, hm_ref, scr_ref, cls_ref) = refs[:n_cast], refs[n_cast:]
    a_refs = [qkv_a_refs[di].at[:, gi * d * MIX_A:(gi + 1) * d * MIX_A]
              for gi in range(3) for di, d in enumerate(DILATIONS)]
    qb_ref = qkv_b_ref.at[:, :MIX_A]
    kb_ref = qkv_b_ref.at[:, MIX_A:MIX_A + N_KV_B * LANE]
    vb_ref = qkv_b_ref.at[:, MIX_A + N_KV_B * LANE:]
    i = pl.program_id(0)
    last = pl.num_programs(0) - 1
    cast_work = _cast_work(cast_src, cast_dst)

    def qkv_work(src_ref):
        return _qkv_work(src_ref, gm_ref, wqkv_ref, gq_ref, a_refs, qb_ref, kb_ref, vb_ref,
                         hm_ref, scr_ref, cls_ref)

    def stage1(fillers=()):
        x = x_ref[...]
        _ffn_hidden((x * _rms_scale(x) * g1_ref[...]).astype(BF16), win_ref, act_ref, fillers)
        x1 = x_ref[...] + 0.5 * _ffn_down(act_ref, wout_ref)
        x1_ref[...] = x1
        x1_carry[...] = x1

    @pl.when(i == 0)
    def _():
        stage1(cast_work)

    @pl.when(jnp.logical_and(i > 0, i < last))
    def _():
        x1_work[...] = x1_carry[...]
        stage1(qkv_work(x1_work) + cast_work)

    @pl.when(i == last)
    def _():
        for work in qkv_work(x1_carry) + cast_work:
            work()


def _ffn_qkv(x, g1, w_in, w_out, gm, w_qkv, gains, layer, casts, *, tm=ROW_TILE):
    T, D = x.shape
    nt = T // tm
    d_ff = w_out.shape[0]
    cast_in, cast_out, cast_shapes = _cast_plumbing(casts, nt + 1)
    cur = lambda i: (jnp.minimum(i, nt - 1), 0)
    prev = lambda i: (jnp.maximum(i - 1, 0), 0)
    gain = lambda width: _resident((None, 1, width), lambda i: (layer, 0, 0))
    b_width = MIX_A + 2 * N_KV_B * LANE
    shapes = [(T // d, 3 * d * MIX_A) for d in DILATIONS] + [(T, b_width)]
    blocks = [(tm // d, 3 * d * MIX_A) for d in DILATIONS] + [(tm, b_width)]
    outs = pl.pallas_call(
        functools.partial(_ffn_qkv_kernel, n_cast=len(casts)),
        grid=(nt + 1,),
        in_specs=[
            pl.BlockSpec((tm, D), cur), gain(D), _whole(w_in), _whole(w_out), gain(D),
            _whole(w_qkv), gain(gains.shape[-1]),
        ] + cast_in,
        out_specs=([pl.BlockSpec((tm, D), cur)] + [pl.BlockSpec(b, prev) for b in blocks]
                   + cast_out),
        out_shape=([jax.ShapeDtypeStruct((T, D), F32)]
                   + [jax.ShapeDtypeStruct(s, BF16) for s in shapes] + cast_shapes),
        scratch_shapes=[pltpu.VMEM((tm, D), F32), pltpu.VMEM((tm, D), F32),
                        pltpu.VMEM((tm, d_ff), BF16), pltpu.VMEM((tm, D), BF16),
                        pltpu.VMEM((3 * MIX_A // LANE, tm, LANE), F32),
                        pltpu.VMEM((MIX_A // LANE, tm, LANE), F32)],
        compiler_params=_cparams(("arbitrary",)),
        name="ffn_qkv",
    )(x, g1, w_in, w_out, gm, w_qkv, gains, *[w for w, _ in casts])
    n_d = len(DILATIONS)
    return outs[0], outs[1:1 + n_d], outs[1 + n_d], outs[2 + n_d:]


def _t5_bucket_np(rel):
    half = N_BUCKETS // 2
    max_exact = half // 2
    ret = np.where(rel > 0, half, 0)
    n = np.abs(rel)
    nf = np.maximum(n, 1).astype(np.float64)
    large = max_exact + (np.log(nf / max_exact) / math.log(MAX_DISTANCE / max_exact)
                         * (half - max_exact)).astype(np.int64)
    large = np.minimum(large, half - 1)
    return ret + np.where(n < max_exact, n, large)


def _bucket_tiles(radius, dilation):
    w = QBLOCK + 2 * radius
    key = np.arange(w)[:, None]
    q = np.arange(QBLOCK)[None, :]
    tiles = []
    for off in (0, -radius, -2 * radius):
        rel = off + key - q
        tiles.append(np.where(np.abs(rel) <= radius, _t5_bucket_np(rel * dilation), -1))
    return np.stack(tiles).astype(np.int32)


def _bias_kernel(rb_ref, idx_ref, o_ref, *, head0, buckets):
    idx = idx_ref[...]
    for h in range(N_HEADS):
        tile = jnp.full(idx.shape, NEG, F32)
        for b in buckets:
            tile = jnp.where(idx == b, rb_ref[b, head0 + h] * LOG2E, tile)
        o_ref[h] = tile


def _bias_tiles(rel_bias, radius, dilation, head0):
    idx_np = _bucket_tiles(radius, dilation)
    buckets = tuple(int(b) for b in np.unique(idx_np[idx_np >= 0]))
    nv, w, bq = idx_np.shape
    return pl.pallas_call(
        functools.partial(_bias_kernel, head0=head0, buckets=buckets),
        grid=(nv,),
        in_specs=[
            pl.BlockSpec(memory_space=pltpu.SMEM),
            pl.BlockSpec((None, w, bq), lambda v: (v, 0, 0)),
        ],
        out_specs=pl.BlockSpec((None, N_HEADS, w, bq), lambda v: (v, 0, 0, 0)),
        out_shape=jax.ShapeDtypeStruct((nv, N_HEADS, w, bq), F32),
        compiler_params=_cparams(("arbitrary",)),
        name="bias_tiles",
    )(rel_bias, jnp.asarray(idx_np))


def _window(i, nb, radius, length):
    w = QBLOCK + 2 * radius
    start = pl.multiple_of(jnp.clip(i * QBLOCK - radius, 0, length - w), radius)
    var = jnp.where(i == 0, 0, jnp.where(i == nb - 1, 2, 1))
    return start, var


_NT = (((1,), (1,)), ((), ()))


def _pair_attention(qp, kp, vp, bias_lo, bias_hi, sink_lo=None, sink_hi=None):
    lane = lax.broadcasted_iota(jnp.int32, qp.shape, 1)
    zero = jnp.zeros_like(qp)
    vt = vp.T
    ones = jnp.ones((HEAD_DIM, vt.shape[1]), vt.dtype)
    outs, lses = [], []
    for hi, (bias, sink) in enumerate(((bias_lo, sink_lo), (bias_hi, sink_hi))):
        keep = (lane >= HEAD_DIM) if hi else (lane < HEAD_DIM)
        s = lax.dot_general(kp, jnp.where(keep, qp, zero), _NT, preferred_element_type=F32) + bias
        m = jnp.max(s, axis=0, keepdims=True)
        if sink is not None:
            m = jnp.maximum(m, sink)
        p = jnp.exp2(s - m).astype(BF16)
        vt_den = jnp.concatenate([ones, vt[HEAD_DIM:]] if hi else [vt[:HEAD_DIM], ones], axis=0)
        o = jnp.dot(vt_den, p, preferred_element_type=F32)
        den = o[:1] if hi else o[HEAD_DIM:HEAD_DIM + 1]
        if sink is not None:
            den = den + jnp.exp2(sink - m)
        outs.append((o[HEAD_DIM:] if hi else o[:HEAD_DIM]) * (1.0 / den))
        lses.append(m + jnp.log2(den))
    o_t = jnp.concatenate(outs, axis=0)
    return o_t.astype(BF16).T, lses[0], lses[1]


def _attn_a_kernel(q_ref, k_ref, v_ref, bias_ref, o_ref, lse_ref, *,
                   radius, length, nb, group, classes):
    w = QBLOCK + 2 * radius
    for cl, g in [(cl, g) for cl in range(classes) for g in range(group)]:
        start, var = _window(pl.program_id(2) * group + g, nb, radius, length)
        rows = slice(g * QBLOCK, (g + 1) * QBLOCK)
        lses = []
        for pr in range(N_HEADS // 2):
            cols = slice(cl * MIX_A + pr * LANE, cl * MIX_A + (pr + 1) * LANE)
            o, lse_lo, lse_hi = _pair_attention(
                q_ref[rows, cols], k_ref[pl.ds(start, w), cols], v_ref[pl.ds(start, w), cols],
                bias_ref[var, 2 * pr], bias_ref[var, 2 * pr + 1])
            o_ref[rows, cols] = o
            lses += [lse_lo, lse_hi]
        lse_t = jnp.concatenate([jnp.broadcast_to(l, (LSE_LANES, QBLOCK)) for l in lses], axis=0)
        lse_ref[rows, cl * LANE:(cl + 1) * LANE] = lse_t.T


def _attn_a(qkv, bias, batch, *, window, dilation):
    d = dilation
    L = qkv.shape[0] // batch
    nb = L // QBLOCK
    radius = window // (2 * d)
    group = min(nb, ATTN_GROUP)
    classes = min(d, ATTN_GROUP // group)
    n_cg = d // classes
    qkv = qkv.reshape(batch, L, qkv.shape[-1])
    o, lse = pl.pallas_call(
        functools.partial(_attn_a_kernel, radius=radius, length=L, nb=nb, group=group,
                          classes=classes),
        grid=(batch, n_cg, nb // group),
        in_specs=[
            pl.BlockSpec((None, group * QBLOCK, classes * MIX_A), lambda b, r, i: (b, i, r)),
            pl.BlockSpec((None, L, classes * MIX_A), lambda b, r, i: (b, 0, n_cg + r)),
            pl.BlockSpec((None, L, classes * MIX_A), lambda b, r, i: (b, 0, 2 * n_cg + r)),
            _resident(bias.shape, lambda b, r, i: (0, 0, 0, 0)),
        ],
        out_specs=[
            pl.BlockSpec((None, group * QBLOCK, classes * MIX_A), lambda b, r, i: (b, i, r)),
            pl.BlockSpec((None, group * QBLOCK, classes * LANE), lambda b, r, i: (b, i, r)),
        ],
        out_shape=[
            jax.ShapeDtypeStruct((batch, L, d * MIX_A), BF16),
            jax.ShapeDtypeStruct((batch, L, d * LANE), F32),
        ],
        compiler_params=_cparams(("parallel", "parallel", "arbitrary")),
        name=f"attn_a_d{d}",
    )(qkv, qkv, qkv, bias)
    return o.reshape(batch * L, d * MIX_A), lse.reshape(batch * L, d * LANE)


def _attn_b_kernel(sink_ref, q_ref, k_ref, v_ref, bias_ref, o_ref, *, radius, length, nb, group):
    w = QBLOCK + 2 * radius
    for g in range(group):
        start, var = _window(pl.program_id(1) * group + g, nb, radius, length)
        rows = slice(g * QBLOCK, (g + 1) * QBLOCK)
        for pr in range(N_HEADS // 2):
            cols = slice(pr * LANE, (pr + 1) * LANE)
            kv_cols = slice(2 * pr // GROUP_B * LANE, (2 * pr // GROUP_B + 1) * LANE)
            lo, hi = 2 * pr, 2 * pr + 1
            o, _, _ = _pair_attention(
                q_ref[rows, cols], k_ref[pl.ds(start, w), kv_cols], v_ref[pl.ds(start, w), kv_cols],
                bias_ref[var, lo], bias_ref[var, hi], sink_ref[lo] * LOG2E, sink_ref[hi] * LOG2E)
            o_ref[rows, cols] = o


def _attn_b(qkv, bias, sink, batch):
    S = qkv.shape[0] // batch
    nb = S // QBLOCK
    group = min(nb, ATTN_GROUP)
    kv_width = N_KV_B * LANE
    k_block = MIX_A // kv_width
    qkv = qkv.reshape(batch, S, qkv.shape[-1])
    o = pl.pallas_call(
        functools.partial(_attn_b_kernel, radius=SWA_RADIUS, length=S, nb=nb, group=group),
        grid=(batch, nb // group),
        in_specs=[
            pl.BlockSpec(memory_space=pltpu.SMEM),
            pl.BlockSpec((None, group * QBLOCK, MIX_A), lambda b, i: (b, i, 0)),
            pl.BlockSpec((None, S, kv_width), lambda b, i: (b, 0, k_block)),
            pl.BlockSpec((None, S, kv_width), lambda b, i: (b, 0, k_block + 1)),
            _resident(bias.shape, lambda b, i: (0, 0, 0, 0)),
        ],
        out_specs=pl.BlockSpec((None, group * QBLOCK, MIX_A), lambda b, i: (b, i, 0)),
        out_shape=jax.ShapeDtypeStruct((batch, S, MIX_A), BF16),
        compiler_params=_cparams(("parallel", "arbitrary")),
        name="attn_b",
    )(sink, qkv, qkv, qkv, bias)
    return o.reshape(batch * S, MIX_A)


def _merge_work(o_refs, l_refs, oscr_ref, lscr_ref, cls_ref, mrg_ref):
    n_d = len(DILATIONS)
    tm = mrg_ref.shape[0]
    n_blk = MIX_A // LANE

    def unpermute(di, d, r):
        d_prev = DILATIONS[di - 1] if di else 1
        a, b = r % d_prev, r // d_prev
        lse = l_refs[di][:, r * LANE:(r + 1) * LANE]
        outs = [o_refs[di][:, r * MIX_A + c * LANE:r * MIX_A + (c + 1) * LANE].astype(F32)
                for c in range(n_blk)]
        if d_prev == 1:
            rows = slice(None) if d == 1 else pl.ds(r, tm // d, stride=d)
            lscr_ref[di, rows, :] = lse
            for c in range(n_blk):
                oscr_ref[di * n_blk + c, rows, :] = outs[c]
        else:
            rows = pl.ds(a * (tm // d_prev) + b, tm // d, stride=d // d_prev)
            cls_ref[n_blk, rows, :] = lse
            for c in range(n_blk):
                cls_ref[c, rows, :] = outs[c]

    def regroup(di, a):
        d_prev = DILATIONS[di - 1]
        seg = slice(a * (tm // d_prev), (a + 1) * (tm // d_prev))
        rows = pl.ds(a, tm // d_prev, stride=d_prev)
        lscr_ref[di, rows, :] = cls_ref[n_blk, seg, :]
        for c in range(n_blk):
            oscr_ref[di * n_blk + c, rows, :] = cls_ref[c, seg, :]

    def weights():
        lse = [lscr_ref[di] for di in range(n_d)]
        mx = functools.reduce(jnp.maximum, lse)
        ex = [jnp.exp2(l - mx) for l in lse]
        inv = 1.0 / functools.reduce(jnp.add, ex)
        for di, e in enumerate(ex):
            lscr_ref[di] = e * inv

    def merge(c):
        lane = lax.broadcasted_iota(jnp.int32, (tm, LANE), 1)
        lo, hi = 2 * c * LSE_LANES, (2 * c + 1) * LSE_LANES
        merged = jnp.zeros((tm, LANE), F32)
        for di in range(n_d):
            wt = lscr_ref[di]
            wfull = jnp.where(lane < HEAD_DIM, wt[:, lo:lo + 1], wt[:, hi:hi + 1])
            merged = merged + wfull * oscr_ref[di * n_blk + c]
        mrg_ref[:, c * LANE:(c + 1) * LANE] = merged.astype(BF16)

    work = []
    for di, d in enumerate(DILATIONS):
        work += [functools.partial(unpermute, di, d, r) for r in range(d)]
        if di and DILATIONS[di - 1] > 1:
            work += [functools.partial(regroup, di, a) for a in range(DILATIONS[di - 1])]
    return work + [weights] + [functools.partial(merge, c) for c in range(n_blk)]


def _mix_ffn_ple_kernel(x_ref, *refs, n_cast):
    n_d = len(DILATIONS)
    o_refs, l_refs, refs = refs[:n_d], refs[n_d:2 * n_d], refs[2 * n_d:]
    (ob_ref, p_ref, wo_ref, g2_ref, win_ref, wout_ref, gp_ref, wg_ref, wp_ref), refs = (
        refs[:9], refs[9:])
    cast_src, out_ref, cast_dst, refs = (refs[:n_cast], refs[n_cast],
                                         refs[n_cast + 1:2 * n_cast + 1], refs[2 * n_cast + 1:])
    x2_carry, h_carry, x2_work, h_work, act_ref, oscr_ref, lscr_ref, cls_ref, mrg_ref = refs

    i = pl.program_id(0)
    last = pl.num_programs(0) - 1
    cast_work = _cast_work(cast_src, cast_dst)

    def merge_work():
        return _merge_work(o_refs, l_refs, oscr_ref, lscr_ref, cls_ref, mrg_ref)

    def project():
        x2 = (x_ref[...]
              + jnp.dot(mrg_ref[...], wo_ref[:MIX_A, :], preferred_element_type=F32)
              + jnp.dot(ob_ref[...], wo_ref[MIX_A:, :], preferred_element_type=F32))
        x2_carry[...] = x2
        h_carry[...] = (x2 * _rms_scale(x2) * g2_ref[...]).astype(BF16)

    def stage2(x2_ref, h_ref, fillers=(), between=lambda: None):
        _ffn_hidden(h_ref[...], win_ref, act_ref, fillers)
        x3 = x2_ref[...] + 0.5 * _ffn_down(act_ref, wout_ref)
        hp = (x3 * _rms_scale(x3) * gp_ref[...]).astype(BF16)
        between()
        gate = jax.nn.sigmoid(jnp.dot(hp, wg_ref[...], preferred_element_type=F32))
        proj = jnp.dot(p_ref[...].astype(BF16), wp_ref[...], preferred_element_type=F32)
        out_ref[...] = x3 + gate * proj

    @pl.when(i == 0)
    def _():
        for work in merge_work() + cast_work:
            work()
        project()

    @pl.when(jnp.logical_and(i > 0, i < last))
    def _():
        x2_work[...] = x2_carry[...]
        h_work[...] = h_carry[...]
        stage2(x2_work, h_work, merge_work() + cast_work, project)

    @pl.when(i == last)
    def _():
        stage2(x2_carry, h_carry, cast_work)


def _mix_ffn_ple(x, outs, lses, ob, p, w_o, g2, w_in, w_out, gp, w_gate, w_proj, layer, casts,
                 *, tm=ROW_TILE):
    T, D = x.shape
    nt = T // tm
    d_ff = w_out.shape[0]
    n_d = len(DILATIONS)
    cast_in, cast_out, cast_shapes = _cast_plumbing(casts, nt + 1)
    cur = lambda i: (jnp.minimum(i, nt - 1), 0)
    prev = lambda i: (jnp.maximum(i - 1, 0), 0)
    gain = lambda: _resident((None, 1, D), lambda i: (layer, 0, 0))
    res = pl.pallas_call(
        functools.partial(_mix_ffn_ple_kernel, n_cast=len(casts)),
        grid=(nt + 1,),
        in_specs=([pl.BlockSpec((tm, D), cur)]
                  + [pl.BlockSpec((tm // d, d * MIX_A), cur) for d in DILATIONS]
                  + [pl.BlockSpec((tm // d, d * LANE), cur) for d in DILATIONS]
                  + [pl.BlockSpec((tm, MIX_A), cur),
                     pl.BlockSpec((None, tm, p.shape[-1]),
                                  lambda i: (layer, jnp.maximum(i - 1, 0), 0)),
                     _whole(w_o), gain(), _whole(w_in), _whole(w_out), gain(), _whole(w_gate),
                     _layer_block(w_proj, layer)]
                  + cast_in),
        out_specs=[pl.BlockSpec((tm, D), prev)] + cast_out,
        out_shape=[jax.ShapeDtypeStruct((T, D), F32)] + cast_shapes,
        scratch_shapes=[pltpu.VMEM((tm, D), F32), pltpu.VMEM((tm, D), BF16),
                        pltpu.VMEM((tm, D), F32), pltpu.VMEM((tm, D), BF16),
                        pltpu.VMEM((tm, d_ff), BF16),
                        pltpu.VMEM((n_d * MIX_A // LANE, tm, LANE), F32),
                        pltpu.VMEM((n_d, tm, LANE), F32),
                        pltpu.VMEM((MIX_A // LANE + 1, tm, LANE), F32),
                        pltpu.VMEM((tm, MIX_A), BF16)],
        compiler_params=_cparams(("arbitrary",)),
        name="mix_ffn_ple",
    )(x, *outs, *lses, ob, p, w_o, g2, w_in, w_out, gp, w_gate, w_proj, *[w for w, _ in casts])
    return res[0], res[1:]


def kernel(x, p, rel_bias, norm_ffn1, ffn1_w_in, ffn1_w_out, norm_mix, w_qkv, q_norm_a, k_norm_a, q_norm_b, k_norm_b, sink_b, w_o, norm_ffn2, ffn2_w_in, ffn2_w_out, norm_ple, w_ple_gate, w_ple_proj):
    B, S, D = x.shape
    depth = p.shape[0]
    T = B * S
    x = x.reshape(T, D)
    p = p.reshape(depth, T, p.shape[-1])

    bias_a = [_bias_tiles(rel_bias, window // (2 * d), d, 0) for window, d in DILATED_CONFIGS]
    bias_b = _bias_tiles(rel_bias, SWA_RADIUS, 1, N_HEADS)

    scale = HEAD_DIM ** -0.5 * LOG2E
    tile = lambda g, n: jnp.tile(g, (1, n))
    gains = jnp.concatenate([
        tile(q_norm_a * scale, N_HEADS), tile(k_norm_a, N_HEADS),
        jnp.ones((depth, MIX_A), F32),
        tile(q_norm_b * scale, N_HEADS), tile(k_norm_b, N_KV_B),
        jnp.ones((depth, KV_B), F32)], axis=1)[:, None]
    row = lambda g: g[:, None]
    first_half = (ffn1_w_in, ffn1_w_out, w_qkv)
    second_half = (w_o, ffn2_w_in, ffn2_w_out, w_ple_gate)
    w_first = [w[0].astype(BF16) for w in first_half]
    w_proj_b = w_ple_proj.astype(BF16)

    for i in range(depth):
        x, qkv_a, qkv_b, w_second = _ffn_qkv(
            x, row(norm_ffn1), w_first[0], w_first[1], row(norm_mix), w_first[2], gains, i,
            [(w, i) for w in second_half])
        outs, lses = [], []
        for di, (window, d) in enumerate(DILATED_CONFIGS):
            o, lse = _attn_a(qkv_a[di], bias_a[di], B, window=window, dilation=d)
            outs.append(o)
            lses.append(lse)
        ob = _attn_b(qkv_b, bias_b, sink_b[i], B)
        w_o_b, w_in2, w_out2, w_gate_b = w_second
        x, w_first = _mix_ffn_ple(
            x, outs, lses, ob, p, w_o_b, row(norm_ffn2), w_in2, w_out2, row(norm_ple), w_gate_b,
            w_proj_b, i, [(w, i + 1) for w in first_half] if i + 1 < depth else [])
    return x.reshape(B, S, D)
```
